```python
import math
import jax, jax.numpy as jnp
from jax import lax
import numpy as np

D_MODEL = 2048
BATCH = 4
SEQ = 4096
DEPTH = 4

CHUNK = 64
Q_BLOCK = 128
N_BRANCH = 4
BR_WIDTH = 1024
GMLP_BLOCK = 128
GMLP_GROUPS = 4
GMLP_GC = BR_WIDTH // GMLP_GROUPS
DIFF_HEADS = 8
DIFF_D = 64
DIFF_VD = 2 * DIFF_D
DIFF_QK = DIFF_HEADS * 2 * DIFF_D
DIFF_V = DIFF_HEADS * DIFF_VD
CONV_K = 31
SB_HEADS = 8
SB_D = 128
SB_W = SB_HEADS * SB_D
ROT_DIM = DIFF_D // 4
ROPE_THETA = 500000.0
FF_DIM = 5632
ALPHA = (2 * DEPTH) ** 0.25
BETA = (8 * DEPTH) ** -0.25
LN_EPS = 1e-5
IN_SIZES = (BR_WIDTH, BR_WIDTH,
            DIFF_QK, DIFF_QK, DIFF_V,
            BR_WIDTH, BR_WIDTH,
            SB_W, SB_W, SB_W,
            N_BRANCH * D_MODEL)
IN_COLS = sum(IN_SIZES)

kernel_name = 'hybrid_gated_streaming_encoder'


def layer_norm(x, g, b):
    xf = x.astype(jnp.float32)
    mu = jnp.mean(xf, axis=-1, keepdims=True)
    var = jnp.mean(jnp.square(xf - mu), axis=-1, keepdims=True)
    y = (xf - mu) * lax.rsqrt(var + LN_EPS)
    return (y * g + b).astype(x.dtype)


def rms_norm(x, g):
    xf = x.astype(jnp.float32)
    y = xf * lax.rsqrt(jnp.mean(jnp.square(xf), axis=-1, keepdims=True) + LN_EPS)
    return (y * g).astype(x.dtype)


def swiglu_ffn(x, w_in, w_out):
    a, gt = jnp.split(x @ w_in, 2, axis=-1)
    return (jax.nn.silu(a) * gt) @ w_out


def chunk_mask(t_idx, s_idx):
    return (s_idx // CHUNK)[None, :] <= (t_idx // CHUNK)[:, None]


def to_blocks(a):
    b, s = a.shape[:2]
    return a.reshape((b, s // Q_BLOCK, Q_BLOCK) + a.shape[2:]).swapaxes(0, 1)


def from_blocks(a):
    a = a.swapaxes(0, 1)
    return a.reshape((a.shape[0], a.shape[1] * a.shape[2]) + a.shape[3:])


def partial_rope(x, cos, sin):
    half = ROT_DIM // 2
    cos = cos.astype(x.dtype)
    sin = sin.astype(x.dtype)
    x1 = x[..., :half]
    x2 = x[..., half:ROT_DIM]
    return jnp.concatenate([x1 * cos - x2 * sin, x2 * cos + x1 * sin, x[..., ROT_DIM:]], axis=-1)


def gmlp_spatial_gating(u, v, ln_g, ln_b, w_s, b_s):
    b, s, _ = v.shape
    v = layer_norm(v, ln_g, ln_b)
    vb = v.reshape(b, s // GMLP_BLOCK, GMLP_BLOCK, GMLP_GROUPS, GMLP_GC)
    pos = jnp.arange(GMLP_BLOCK)
    w = jnp.where(chunk_mask(pos, pos)[None], w_s, 0.0)
    mixed = jnp.einsum('gts,bnsgc->bntgc', w, vb) + b_s.T[None, None, :, :, None]
    return u * mixed.reshape(b, s, BR_WIDTH)


def differential_attention(q, k, v, lam, lam_init, norm_g):
    s = q.shape[1]
    s_idx = jnp.arange(s)
    scale = DIFF_D ** -0.5

    def block(args):
        qb, t_idx = args
        sc = jnp.einsum('bthnd,bshnd->bnhts', qb, k).astype(jnp.float32) * scale
        sc = jnp.where(chunk_mask(t_idx, s_idx), sc, -jnp.inf)
        p = jax.nn.softmax(sc, axis=-1)
        a = p[:, 0] - lam * p[:, 1]
        return jnp.einsum('bhts,bshe->bthe', a.astype(v.dtype), v)

    t_blocks = jnp.arange(s).reshape(-1, Q_BLOCK)
    o = from_blocks(lax.map(block, (to_blocks(q), t_blocks)))
    o = rms_norm(o, norm_g.reshape(DIFF_HEADS, DIFF_VD)) * (1.0 - lam_init)
    return o.reshape(o.shape[0], s, DIFF_V)


def conformer_conv(a, gt, w_dw, b_dw, ln_g, ln_b):
    h = a * jax.nn.sigmoid(gt)
    h = lax.conv_general_dilated(
        h, w_dw[:, None, :], window_strides=(1,), padding=[(CONV_K - 1, 0)],
        dimension_numbers=('NWC', 'WIO', 'NWC'), feature_group_count=BR_WIDTH) + b_dw
    return jax.nn.silu(layer_norm(h, ln_g, ln_b))


def stick_breaking_attention(q, k, v):
    s = q.shape[1]
    s_idx = jnp.arange(s)
    scale = SB_D ** -0.5

    def block(args):
        qb, t_idx = args
        z = jnp.einsum('bthd,bshd->bhts', qb, k).astype(jnp.float32) * scale
        causal = s_idx[None, :] < t_idx[:, None]
        log_rest = jnp.where(causal, jax.nn.log_sigmoid(-z), 0.0)
        suffix = lax.cumsum(log_rest, axis=3, reverse=True) - log_rest
        w = jnp.where(causal, jnp.exp(jax.nn.log_sigmoid(z) + suffix), 0.0)
        return jnp.einsum('bhts,bshd->bthd', w.astype(v.dtype), v)

    t_blocks = jnp.arange(s).reshape(-1, Q_BLOCK)
    o = from_blocks(lax.map(block, (to_blocks(q), t_blocks)))
    return o.reshape(o.shape[0], s, SB_W)


def hybrid_mixer(h, cos, sin, lam_init, w_in, gmlp_ln_g, gmlp_ln_b, gmlp_ws, gmlp_bs,
                 diff_lq1, diff_lk1, diff_lq2, diff_lk2, diff_norm_g,
                 conv_w, conv_b, conv_ln_g, conv_ln_b, w_branch, w_out):
    b, s, _ = h.shape
    points = []
    acc = 0
    for size in IN_SIZES[:-1]:
        acc += size
        points.append(acc)
    (ua, va, dq, dk, dv, ca, cg, sq, sk, sv, gates) = jnp.split(h @ w_in, points, axis=-1)

    o_a = gmlp_spatial_gating(ua, va, gmlp_ln_g, gmlp_ln_b, gmlp_ws, gmlp_bs)

    q = partial_rope(dq.reshape(b, s, DIFF_HEADS * 2, DIFF_D), cos, sin).reshape(b, s, DIFF_HEADS, 2, DIFF_D)
    k = partial_rope(dk.reshape(b, s, DIFF_HEADS * 2, DIFF_D), cos, sin).reshape(b, s, DIFF_HEADS, 2, DIFF_D)
    lam = (jnp.exp(jnp.sum(diff_lq1.astype(jnp.float32) * diff_lk1.astype(jnp.float32)))
           - jnp.exp(jnp.sum(diff_lq2.astype(jnp.float32) * diff_lk2.astype(jnp.float32))) + lam_init)
    o_b = differential_attention(q, k, dv.reshape(b, s, DIFF_HEADS, DIFF_VD), lam, lam_init, diff_norm_g)

    o_c = conformer_conv(ca, cg, conv_w, conv_b, conv_ln_g, conv_ln_b)

    o_d = stick_breaking_attention(sq.reshape(b, s, SB_HEADS, SB_D), sk.reshape(b, s, SB_HEADS, SB_D),
                                   sv.reshape(b, s, SB_HEADS, SB_D))

    branches = jnp.stack([o_a, o_b, o_c, o_d], axis=2)
    proj = jnp.einsum('bsnc,ncd->bsnd', branches, w_branch)
    gate = jax.nn.sigmoid(gates.reshape(b, s, N_BRANCH, D_MODEL))
    merged = jnp.sum(gate * proj, axis=2)
    return merged @ w_out


def setup_inputs(seed: int = 0) -> dict:
    key = jax.random.key(seed)
    ks = jax.random.split(key, 28)
    L = DEPTH

    def nrm(k, shape, scale):
        return scale * jax.random.normal(k, shape, jnp.float32)

    def gain(k, shape):
        return 1.0 + nrm(k, shape, 0.02)

    x = jax.random.normal(ks[0], (BATCH, SEQ, D_MODEL), jnp.float32)
    offsets = jax.random.randint(ks[1], (BATCH, 1), 0, 1024)
    positions = (offsets + jnp.arange(SEQ)[None, :]).astype(jnp.int32)
    return {
        'x': x,
        'positions': positions,
        'ffn1_w_in': nrm(ks[2], (L, D_MODEL, 2 * FF_DIM), D_MODEL ** -0.5),
        'ffn1_w_out': nrm(ks[3], (L, FF_DIM, D_MODEL), BETA * FF_DIM ** -0.5),
        'ln1_g': gain(ks[4], (L, D_MODEL)),
        'ln1_b': nrm(ks[5], (L, D_MODEL), 0.02),
        'w_in': nrm(ks[6], (L, D_MODEL, IN_COLS), D_MODEL ** -0.5),
        'gmlp_ln_g': gain(ks[7], (L, BR_WIDTH)),
        'gmlp_ln_b': nrm(ks[8], (L, BR_WIDTH), 0.02),
        'gmlp_ws': nrm(ks[9], (L, GMLP_GROUPS, GMLP_BLOCK, GMLP_BLOCK), 0.5 * GMLP_BLOCK ** -0.5),
        'gmlp_bs': gain(ks[10], (L, GMLP_GROUPS, GMLP_BLOCK)),
        'diff_lq1': nrm(ks[11], (L, DIFF_D), 0.1),
        'diff_lk1': nrm(ks[12], (L, DIFF_D), 0.1),
        'diff_lq2': nrm(ks[13], (L, DIFF_D), 0.1),
        'diff_lk2': nrm(ks[14], (L, DIFF_D), 0.1),
        'diff_norm_g': gain(ks[15], (L, DIFF_V)),
        'conv_w': nrm(ks[16], (L, CONV_K, BR_WIDTH), CONV_K ** -0.5),
        'conv_b': nrm(ks[17], (L, BR_WIDTH), 0.02),
        'conv_ln_g': gain(ks[18], (L, BR_WIDTH)),
        'conv_ln_b': nrm(ks[19], (L, BR_WIDTH), 0.02),
        'w_branch': nrm(ks[20], (L, N_BRANCH, BR_WIDTH, D_MODEL), BETA * BR_WIDTH ** -0.5),
        'w_out': nrm(ks[21], (L, D_MODEL, D_MODEL), BETA * D_MODEL ** -0.5),
        'ln2_g': gain(ks[22], (L, D_MODEL)),
        'ln2_b': nrm(ks[23], (L, D_MODEL), 0.02),
        'ffn2_w_in': nrm(ks[24], (L, D_MODEL, 2 * FF_DIM), D_MODEL ** -0.5),
        'ffn2_w_out': nrm(ks[25], (L, FF_DIM, D_MODEL), BETA * FF_DIM ** -0.5),
        'ln3_g': gain(ks[26], (L, D_MODEL)),
        'ln3_b': nrm(ks[27], (L, D_MODEL), 0.02),
    }


def reference(x, positions, ffn1_w_in, ffn1_w_out, ln1_g, ln1_b, w_in, gmlp_ln_g, gmlp_ln_b,
              gmlp_ws, gmlp_bs, diff_lq1, diff_lk1, diff_lq2, diff_lk2, diff_norm_g,
              conv_w, conv_b, conv_ln_g, conv_ln_b, w_branch, w_out, ln2_g, ln2_b,
              ffn2_w_in, ffn2_w_out, ln3_g, ln3_b):
    inv_freq = ROPE_THETA ** (-jnp.arange(0, ROT_DIM, 2, dtype=jnp.float32) / ROT_DIM)
    ang = positions.astype(jnp.float32)[..., None] * inv_freq
    cos = jnp.cos(ang)[:, :, None, :]
    sin = jnp.sin(ang)[:, :, None, :]

    for i in range(DEPTH):
        lam_init = 0.8 - 0.6 * math.exp(-0.3 * i)
        x = layer_norm(ALPHA * x + 0.5 * swiglu_ffn(x, ffn1_w_in[i], ffn1_w_out[i]), ln1_g[i], ln1_b[i])
        mix = hybrid_mixer(x, cos, sin, lam_init, w_in[i], gmlp_ln_g[i], gmlp_ln_b[i], gmlp_ws[i], gmlp_bs[i],
                           diff_lq1[i], diff_lk1[i], diff_lq2[i], diff_lk2[i], diff_norm_g[i],
                           conv_w[i], conv_b[i], conv_ln_g[i], conv_ln_b[i], w_branch[i], w_out[i])
        x = layer_norm(ALPHA * x + mix, ln2_g[i], ln2_b[i])
        x = layer_norm(ALPHA * x + 0.5 * swiglu_ffn(x, ffn2_w_in[i], ffn2_w_out[i]), ln3_g[i], ln3_b[i])
    return x
```

```python
import functools
import math

import jax
import jax.numpy as jnp
from jax import lax
from jax.experimental import pallas as pl
from jax.experimental.pallas import tpu as pltpu

F32 = jnp.float32
BF16 = jnp.bfloat16

D_MODEL = 2048
FF_DIM = 5632
CHUNK = 64
N_BRANCH = 4
BR_WIDTH = 1024
GMLP_BLOCK = 128
GMLP_GROUPS = 4
GMLP_GC = BR_WIDTH // GMLP_GROUPS
DIFF_HEADS = 8
DIFF_D = 64
DIFF_VD = 2 * DIFF_D
SB_HEADS = 8
SB_D = 128
CONV_K = 31
ROT_DIM = DIFF_D // 4
ROPE_THETA = 500000.0
MODEL_DEPTH = 4
ALPHA = (2 * MODEL_DEPTH) ** 0.25
LN_EPS = 1e-5
COL_UV = 0
COL_DQK = 2 * BR_WIDTH
COL_DV = 4 * BR_WIDTH
COL_CONV = 5 * BR_WIDTH
COL_SB = 7 * BR_WIDTH
COL_GATE = 10 * BR_WIDTH

LANES = 128
VMEM_LIMIT_BYTES = 56 * 1024 * 1024

FFN_TM = 512
FFN_TF = 512
PROJ_TM = 1024
PROJ_TN = 1024
PROJ_ROPE_TM = 512
MERGE_TM = 512
MERGE_TN = 1024
OUT_TM = 512
GMLP_TM = 512
CONV_TT = 256
CONV_HALO = 32
ATT_TQ = 256
ROPE_TM = 1024


def _params(*sem):
    return pltpu.CompilerParams(dimension_semantics=sem, vmem_limit_bytes=VMEM_LIMIT_BYTES)


def _sigmoid(x):
    return 1.0 / (1.0 + jnp.exp(-x))


def _chunk_of(size, axis):
    return lax.broadcasted_iota(jnp.int32, (size, size), axis) >> int(math.log2(CHUNK))


def _layer_norm_rows(y, g, b):
    mu = jnp.mean(y, axis=-1, keepdims=True)
    yc = y - mu
    var = jnp.mean(yc * yc, axis=-1, keepdims=True)
    return yc * lax.rsqrt(var + LN_EPS) * g + b


def _ffn_kernel(x_ref, wa_ref, wg_ref, wo_ref, g_ref, b_ref, of_ref, ob_ref, xb_ref, acc_ref):
    j = pl.program_id(1)

    @pl.when(j == 0)
    def _():
        xb_ref[...] = x_ref[...].astype(BF16)

    xb = xb_ref[...]
    a = jnp.dot(xb, wa_ref[...], preferred_element_type=F32)
    gt = jnp.dot(xb, wg_ref[...], preferred_element_type=F32)
    act = (a * _sigmoid(a) * gt).astype(BF16)
    contrib = jnp.dot(act, wo_ref[...], preferred_element_type=F32)

    @pl.when(j == 0)
    def _():
        acc_ref[...] = contrib

    @pl.when(j > 0)
    def _():
        acc_ref[...] += contrib

    @pl.when(j == pl.num_programs(1) - 1)
    def _():
        y = ALPHA * x_ref[...] + 0.5 * acc_ref[...]
        out = _layer_norm_rows(y, g_ref[...], b_ref[...])
        of_ref[...] = out
        ob_ref[...] = out.astype(BF16)


def _ffn(x, w_in, w_out, ln_g, ln_b):
    n, d = x.shape
    f = w_out.shape[0]
    tm = min(FFN_TM, n)
    tf = FFN_TF
    nf = f // tf
    return pl.pallas_call(
        _ffn_kernel,
        grid=(n // tm, nf),
        in_specs=[
            pl.BlockSpec((tm, d), lambda i, j: (i, 0)),
            pl.BlockSpec((d, tf), lambda i, j: (0, j)),
            pl.BlockSpec((d, tf), lambda i, j: (0, j + nf)),
            pl.BlockSpec((tf, d), lambda i, j: (j, 0)),
            pl.BlockSpec((1, d), lambda i, j: (0, 0)),
            pl.BlockSpec((1, d), lambda i, j: (0, 0)),
        ],
        out_specs=[
            pl.BlockSpec((tm, d), lambda i, j: (i, 0)),
            pl.BlockSpec((tm, d), lambda i, j: (i, 0)),
        ],
        out_shape=[jax.ShapeDtypeStruct((n, d), F32), jax.ShapeDtypeStruct((n, d), BF16)],
        scratch_shapes=[pltpu.VMEM((tm, d), BF16), pltpu.VMEM((tm, d), F32)],
        compiler_params=_params("parallel", "arbitrary"),
        name="ffn",
    )(x, w_in, w_in, w_out, ln_g.reshape(1, d), ln_b.reshape(1, d))


def _rope_table_kernel(pos_ref, invf_ref, c_ref, s1_ref, s2_ref):
    ang = pos_ref[...] * invf_ref[...]
    cos = jnp.cos(ang)
    sin = jnp.sin(ang)
    lane = lax.broadcasted_iota(jnp.int32, ang.shape, 1) & (DIFF_D - 1)
    half = ROT_DIM // 2
    c_ref[...] = jnp.where(lane < ROT_DIM, cos, 1.0)
    s1_ref[...] = jnp.where(lane < half, -sin, 0.0)
    s2_ref[...] = jnp.where((lane >= half) & (lane < ROT_DIM), sin, 0.0)


def _rope_tables(positions):
    n = positions.size
    tm = min(ROPE_TM, n)
    half = ROT_DIM // 2
    inv_freq = ROPE_THETA ** (-jnp.arange(0, ROT_DIM, 2, dtype=F32) / ROT_DIM)
    per_head = jnp.concatenate([inv_freq, inv_freq, jnp.zeros((DIFF_D - ROT_DIM,), F32)])
    invf = jnp.tile(per_head, LANES // DIFF_D).reshape(1, LANES)
    pos = positions.reshape(n, 1).astype(F32)
    tab = jax.ShapeDtypeStruct((n, LANES), F32)
    return pl.pallas_call(
        _rope_table_kernel,
        grid=(n // tm,),
        in_specs=[pl.BlockSpec((tm, 1), lambda i: (i, 0)), pl.BlockSpec((1, LANES), lambda i: (0, 0))],
        out_specs=[pl.BlockSpec((tm, LANES), lambda i: (i, 0))] * 3,
        out_shape=[tab, tab, tab],
        compiler_params=_params("parallel"),
        name="rope_tables",
    )(pos, invf)


def _proj_kernel(x_ref, w_ref, o_ref):
    o_ref[...] = jnp.dot(x_ref[...], w_ref[...], preferred_element_type=F32).astype(o_ref.dtype)


def _proj_rope_kernel(x_ref, w_ref, c_ref, s1_ref, s2_ref, o_ref, *, q_tiles, q_scale):
    acc = jnp.dot(x_ref[...], w_ref[...], preferred_element_type=F32)
    tn = acc.shape[1]
    reps = tn // LANES
    c = jnp.concatenate([c_ref[...]] * reps, axis=1)
    s1 = jnp.concatenate([s1_ref[...]] * reps, axis=1)
    s2 = jnp.concatenate([s2_ref[...]] * reps, axis=1)
    half = ROT_DIM // 2
    nxt = pltpu.roll(acc, tn - half, axis=1)
    prv = pltpu.roll(acc, half, axis=1)
    out = acc * c + nxt * s1 + prv * s2
    scale = jnp.where(pl.program_id(1) < q_tiles, q_scale, 1.0).astype(F32)
    o_ref[...] = (out * scale).astype(o_ref.dtype)


def _proj(xb, w, col0, ncols, out_dtype, rope=None):
    n, d = xb.shape
    tm = min(PROJ_TM if rope is None else PROJ_ROPE_TM, n)
    tn = PROJ_TN
    off = col0 // tn
    x_spec = pl.BlockSpec((tm, d), lambda i, j: (i, 0))
    w_spec = pl.BlockSpec((d, tn), lambda i, j: (0, j + off))
    o_spec = pl.BlockSpec((tm, tn), lambda i, j: (i, j))
    common = dict(
        grid=(n // tm, ncols // tn),
        out_specs=o_spec,
        out_shape=jax.ShapeDtypeStruct((n, ncols), out_dtype),
        compiler_params=_params("parallel", "arbitrary"),
    )
    if rope is None:
        return pl.pallas_call(_proj_kernel, in_specs=[x_spec, w_spec], name="proj", **common)(xb, w)
    t_spec = pl.BlockSpec((tm, LANES), lambda i, j: (i, 0))
    kern = functools.partial(_proj_rope_kernel, q_tiles=(ncols // 2) // tn, q_scale=DIFF_D ** -0.5)
    return pl.pallas_call(kern, in_specs=[x_spec, w_spec, t_spec, t_spec, t_spec], name="proj_rope",
                          **common)(xb, w, *rope)


def _gmlp_kernel(u_ref, v_ref, g_ref, b_ref, ws_ref, bs_ref, o_ref):
    tm = u_ref.shape[0]
    visible = _chunk_of(GMLP_BLOCK, 1) <= _chunk_of(GMLP_BLOCK, 0)
    g = g_ref[...]
    b = b_ref[...]
    for blk in range(tm // GMLP_BLOCK):
        rows = pl.ds(blk * GMLP_BLOCK, GMLP_BLOCK)
        vn = _layer_norm_rows(v_ref[rows, :], g, b).astype(BF16)
        for grp in range(GMLP_GROUPS):
            cols = pl.ds(grp * GMLP_GC, GMLP_GC)
            w = jnp.where(visible, ws_ref[grp], 0.0).astype(BF16)
            mixed = jnp.dot(w, vn[:, grp * GMLP_GC:(grp + 1) * GMLP_GC], preferred_element_type=F32)
            mixed = mixed + bs_ref[:, grp:grp + 1]
            o_ref[rows, cols] = (u_ref[rows, cols] * mixed).astype(o_ref.dtype)


def _gmlp(uv, ln_g, ln_b, ws, bs):
    n = uv.shape[0]
    tm = min(GMLP_TM, n)
    return pl.pallas_call(
        _gmlp_kernel,
        grid=(n // tm,),
        in_specs=[
            pl.BlockSpec((tm, BR_WIDTH), lambda i: (i, 0)),
            pl.BlockSpec((tm, BR_WIDTH), lambda i: (i, 1)),
            pl.BlockSpec((1, BR_WIDTH), lambda i: (0, 0)),
            pl.BlockSpec((1, BR_WIDTH), lambda i: (0, 0)),
            pl.BlockSpec((GMLP_GROUPS, GMLP_BLOCK, GMLP_BLOCK), lambda i: (0, 0, 0)),
            pl.BlockSpec((GMLP_BLOCK, GMLP_GROUPS), lambda i: (0, 0)),
        ],
        out_specs=pl.BlockSpec((tm, BR_WIDTH), lambda i: (i, 0)),
        out_shape=jax.ShapeDtypeStruct((n, BR_WIDTH), BF16),
        compiler_params=_params("parallel"),
        name="gmlp",
    )(uv, uv, ln_g.reshape(1, -1), ln_b.reshape(1, -1), ws, bs.T)


def _diff_attn_kernel(q_ref, k_ref, v_ref, lq1_ref, lk1_ref, lq2_ref, lk2_ref, ng_ref, o_ref,
                      m_ref, l_ref, acc_ref, *, lam_init):
    qi = pl.program_id(2)
    tq = q_ref.shape[0]
    q = q_ref[...]
    lane = lax.broadcasted_iota(jnp.int32, q.shape, 1)
    qs = (jnp.where(lane < DIFF_D, q, 0), jnp.where(lane >= DIFF_D, q, 0))

    m_ref[...] = jnp.full(m_ref.shape, -1e30, F32)
    l_ref[...] = jnp.zeros(l_ref.shape, F32)
    acc_ref[...] = jnp.zeros(acc_ref.shape, F32)

    def step(j, visible):
        rows = pl.ds(pl.multiple_of(j * tq, tq), tq)
        kb = k_ref[rows, :]
        vb = v_ref[rows, :]
        for mp in range(2):
            s = lax.dot_general(qs[mp], kb, (((1,), (1,)), ((), ())), preferred_element_type=F32)
            if visible is not None:
                s = jnp.where(visible, s, -jnp.inf)
            m_old = m_ref[mp]
            m_new = jnp.maximum(m_old, jnp.max(s, axis=-1, keepdims=True))
            p = jnp.exp(s - m_new)
            corr = jnp.exp(m_old - m_new)
            l_ref[mp] = corr * l_ref[mp] + jnp.sum(p, axis=-1, keepdims=True)
            acc_ref[mp] = corr * acc_ref[mp] + jnp.dot(p.astype(BF16), vb, preferred_element_type=F32)
            m_ref[mp] = m_new

    def body(j, carry):
        step(j, None)
        return carry

    lax.fori_loop(0, qi, body, 0)
    step(qi, _chunk_of(tq, 1) <= _chunk_of(tq, 0))

    lam = (jnp.exp(jnp.sum(lq1_ref[...] * lk1_ref[...], axis=-1, keepdims=True))
           - jnp.exp(jnp.sum(lq2_ref[...] * lk2_ref[...], axis=-1, keepdims=True)) + lam_init)
    o = acc_ref[0] / l_ref[0] - lam * (acc_ref[1] / l_ref[1])
    o = o * lax.rsqrt(jnp.mean(o * o, axis=-1, keepdims=True) + LN_EPS) * ng_ref[...]
    o_ref[...] = (o * (1.0 - lam_init)).astype(o_ref.dtype)


def _diff_attn(dqk, dv, lq1, lk1, lq2, lk2, norm_g, batch, seq, lam_init):
    n = dqk.shape[0]
    tq = min(ATT_TQ, seq)
    nq = seq // tq
    vec = pl.BlockSpec((1, DIFF_D), lambda b, h, i: (0, 0))
    return pl.pallas_call(
        functools.partial(_diff_attn_kernel, lam_init=lam_init),
        grid=(batch, DIFF_HEADS, nq),
        in_specs=[
            pl.BlockSpec((tq, DIFF_VD), lambda b, h, i: (b * nq + i, h)),
            pl.BlockSpec((seq, DIFF_VD), lambda b, h, i: (b, DIFF_HEADS + h)),
            pl.BlockSpec((seq, DIFF_VD), lambda b, h, i: (b, h)),
            vec, vec, vec, vec,
            pl.BlockSpec((1, DIFF_VD), lambda b, h, i: (0, h)),
        ],
        out_specs=pl.BlockSpec((tq, DIFF_VD), lambda b, h, i: (b * nq + i, h)),
        out_shape=jax.ShapeDtypeStruct((n, DIFF_HEADS * DIFF_VD), BF16),
        scratch_shapes=[pltpu.VMEM((2, tq, 1), F32), pltpu.VMEM((2, tq, 1), F32),
                        pltpu.VMEM((2, tq, DIFF_VD), F32)],
        compiler_params=_params("parallel", "parallel", "arbitrary"),
        name="diff_attn",
    )(dqk, dqk, dv, lq1.reshape(1, -1), lk1.reshape(1, -1), lq2.reshape(1, -1), lk2.reshape(1, -1),
      norm_g.reshape(1, -1))


def _conv_kernel(a_ref, g_ref, ah_ref, gh_ref, w_ref, cb_ref, lg_ref, lb_ref, o_ref, h_ref, y_ref,
                 *, tiles_per_seq):
    tt = a_ref.shape[0]
    first = (pl.program_id(0) % tiles_per_seq) == 0
    halo = ah_ref[...] * _sigmoid(gh_ref[...])
    h_ref[pl.ds(0, CONV_HALO), :] = jnp.where(first, 0.0, halo)
    h_ref[pl.ds(CONV_HALO, tt), :] = a_ref[...] * _sigmoid(g_ref[...])
    base = CONV_HALO - (CONV_K - 1)
    for cb in range(a_ref.shape[1] // LANES):
        cols = pl.ds(cb * LANES, LANES)
        acc = jnp.zeros((tt, LANES), F32)
        for r in range(8):
            taps = [k for k in range(CONV_K) if k % 8 == r]
            shifted = h_ref[pl.ds(base + r, tt + taps[-1] - r), cols]
            for k in taps:
                acc = acc + w_ref[pl.ds(k, 1), cols] * shifted[k - r:k - r + tt, :]
        y_ref[:, cols] = acc + cb_ref[:, cols]
    y = _layer_norm_rows(y_ref[...], lg_ref[...], lb_ref[...])
    o_ref[...] = (y * _sigmoid(y)).astype(o_ref.dtype)


def _conv(cacg, w_dw, b_dw, ln_g, ln_b, seq):
    n = cacg.shape[0]
    c = BR_WIDTH
    tt = min(CONV_TT, seq)
    ratio = tt // CONV_HALO
    row = lambda v: v.reshape(1, c)
    return pl.pallas_call(
        functools.partial(_conv_kernel, tiles_per_seq=seq // tt),
        grid=(n // tt,),
        in_specs=[
            pl.BlockSpec((tt, c), lambda i: (i, 0)),
            pl.BlockSpec((tt, c), lambda i: (i, 1)),
            pl.BlockSpec((CONV_HALO, c), lambda i: (jnp.maximum(i * ratio - 1, 0), 0)),
            pl.BlockSpec((CONV_HALO, c), lambda i: (jnp.maximum(i * ratio - 1, 0), 1)),
            pl.BlockSpec((CONV_K, c), lambda i: (0, 0)),
            pl.BlockSpec((1, c), lambda i: (0, 0)),
            pl.BlockSpec((1, c), lambda i: (0, 0)),
            pl.BlockSpec((1, c), lambda i: (0, 0)),
        ],
        out_specs=pl.BlockSpec((tt, c), lambda i: (i, 0)),
        out_shape=jax.ShapeDtypeStruct((n, c), BF16),
        scratch_shapes=[pltpu.VMEM((tt + CONV_HALO, c), F32), pltpu.VMEM((tt, c), F32)],
        compiler_params=_params("parallel"),
        name="conformer_conv",
    )(cacg, cacg, cacg, cacg, w_dw, row(b_dw), row(ln_g), row(ln_b))


def _sb_attn_kernel(q_ref, k_ref, v_ref, o_ref, rest_ref, acc_ref):
    qi = pl.program_id(2)
    tq = q_ref.shape[0]
    q = q_ref[...]
    scale = SB_D ** -0.5
    later = (lax.broadcasted_iota(jnp.int32, (tq, tq), 0)
             > lax.broadcasted_iota(jnp.int32, (tq, tq), 1))
    suffix_ones = jnp.where(later, 1.0, 0.0).astype(BF16)
    causal = later

    rest_ref[...] = jnp.zeros(rest_ref.shape, F32)
    acc_ref[...] = jnp.zeros(acc_ref.shape, F32)

    def step(j, mask):
        rows = pl.ds(pl.multiple_of(j * tq, tq), tq)
        z = lax.dot_general(q, k_ref[rows, :], (((1,), (1,)), ((), ())), preferred_element_type=F32) * scale
        softplus = jnp.maximum(z, 0.0) + jnp.log(1.0 + jnp.exp(-jnp.abs(z)))
        log_rest = -softplus
        if mask is not None:
            log_rest = jnp.where(mask, log_rest, 0.0)
        hi = log_rest.astype(BF16)
        lo = (log_rest - hi.astype(F32)).astype(BF16)
        suffix = (jnp.dot(hi, suffix_ones, preferred_element_type=F32)
                  + jnp.dot(lo, suffix_ones, preferred_element_type=F32))
        rest = rest_ref[...]
        w = jnp.exp((z - softplus) + suffix + rest)
        if mask is not None:
            w = jnp.where(mask, w, 0.0)
        acc_ref[...] += jnp.dot(w.astype(BF16), v_ref[rows, :], preferred_element_type=F32)
        rest_ref[...] = rest + jnp.sum(log_rest, axis=-1, keepdims=True)

    step(qi, causal)

    def body(jj, carry):
        step(qi - 1 - jj, None)
        return carry

    lax.fori_loop(0, qi, body, 0)
    o_ref[...] = acc_ref[...].astype(o_ref.dtype)


def _sb_attn(sqkv, batch, seq):
    n = sqkv.shape[0]
    tq = min(ATT_TQ, seq)
    nq = seq // tq
    return pl.pallas_call(
        _sb_attn_kernel,
        grid=(batch, SB_HEADS, nq),
        in_specs=[
            pl.BlockSpec((tq, SB_D), lambda b, h, i: (b * nq + i, h)),
            pl.BlockSpec((seq, SB_D), lambda b, h, i: (b, SB_HEADS + h)),
            pl.BlockSpec((seq, SB_D), lambda b, h, i: (b, 2 * SB_HEADS + h)),
        ],
        out_specs=pl.BlockSpec((tq, SB_D), lambda b, h, i: (b * nq + i, h)),
        out_shape=jax.ShapeDtypeStruct((n, SB_HEADS * SB_D), BF16),
        scratch_shapes=[pltpu.VMEM((tq, 1), F32), pltpu.VMEM((tq, SB_D), F32)],
        compiler_params=_params("parallel", "parallel", "arbitrary"),
        name="sb_attn",
    )(sqkv, sqkv, sqkv)


def _merge_kernel(h_ref, oa_ref, ob_ref, oc_ref, od_ref, wg_ref, wb_ref, o_ref, acc_ref):
    b = pl.program_id(2)
    gate = _sigmoid(jnp.dot(h_ref[...], wg_ref[...], preferred_element_type=F32))
    for idx, br_ref in enumerate((oa_ref, ob_ref, oc_ref, od_ref)):
        @pl.when(b == idx)
        def _(br_ref=br_ref, first=(idx == 0)):
            term = gate * jnp.dot(br_ref[...], wb_ref[...], preferred_element_type=F32)
            if first:
                acc_ref[...] = term
            else:
                acc_ref[...] += term

    @pl.when(b == N_BRANCH - 1)
    def _():
        o_ref[...] = acc_ref[...].astype(o_ref.dtype)


def _merge(hb, branches, w_in, w_branch):
    n, d = hb.shape
    tm = min(MERGE_TM, n)
    tn = MERGE_TN
    nc = d // tn
    gate_off = COL_GATE // tn
    br_spec = pl.BlockSpec((tm, BR_WIDTH), lambda i, c, b: (i, 0))
    return pl.pallas_call(
        _merge_kernel,
        grid=(n // tm, nc, N_BRANCH),
        in_specs=[
            pl.BlockSpec((tm, d), lambda i, c, b: (i, 0)),
            br_spec, br_spec, br_spec, br_spec,
            pl.BlockSpec((d, tn), lambda i, c, b: (0, gate_off + b * nc + c)),
            pl.BlockSpec((None, BR_WIDTH, tn), lambda i, c, b: (b, 0, c)),
        ],
        out_specs=pl.BlockSpec((tm, tn), lambda i, c, b: (i, c)),
        out_shape=jax.ShapeDtypeStruct((n, d), BF16),
        scratch_shapes=[pltpu.VMEM((tm, tn), F32)],
        compiler_params=_params("parallel", "arbitrary", "arbitrary"),
        name="merge",
    )(hb, *branches, w_in, w_branch)


def _out_ln_kernel(m_ref, w_ref, x_ref, g_ref, b_ref, o_ref):
    y = ALPHA * x_ref[...] + jnp.dot(m_ref[...], w_ref[...], preferred_element_type=F32)
    o_ref[...] = _layer_norm_rows(y, g_ref[...], b_ref[...])


def _out_ln(merged, w_out, x, ln_g, ln_b):
    n, d = x.shape
    tm = min(OUT_TM, n)
    return pl.pallas_call(
        _out_ln_kernel,
        grid=(n // tm,),
        in_specs=[
            pl.BlockSpec((tm, d), lambda i: (i, 0)),
            pl.BlockSpec((d, d), lambda i: (0, 0)),
            pl.BlockSpec((tm, d), lambda i: (i, 0)),
            pl.BlockSpec((1, d), lambda i: (0, 0)),
            pl.BlockSpec((1, d), lambda i: (0, 0)),
        ],
        out_specs=pl.BlockSpec((tm, d), lambda i: (i, 0)),
        out_shape=jax.ShapeDtypeStruct((n, d), F32),
        compiler_params=_params("parallel"),
        name="out_proj_ln",
    )(merged, w_out, x, ln_g.reshape(1, d), ln_b.reshape(1, d))


def kernel(x, positions, ffn1_w_in, ffn1_w_out, ln1_g, ln1_b, w_in, gmlp_ln_g, gmlp_ln_b, gmlp_ws, gmlp_bs,
           diff_lq1, diff_lk1, diff_lq2, diff_lk2, diff_norm_g, conv_w, conv_b, conv_ln_g, conv_ln_b,
           w_branch, w_out, ln2_g, ln2_b, ffn2_w_in, ffn2_w_out, ln3_g, ln3_b):
    batch, seq, d = x.shape
    n = batch * seq
    depth = w_in.shape[0]
    ffn1_w_in, ffn1_w_out, ffn2_w_in, ffn2_w_out, w_in, w_branch, w_out = (
        w.astype(BF16) for w in (ffn1_w_in, ffn1_w_out, ffn2_w_in, ffn2_w_out, w_in, w_branch, w_out))

    rope = _rope_tables(positions)
    xf = x.reshape(n, d)
    for i in range(depth):
        lam_init = 0.8 - 0.6 * math.exp(-0.3 * i)
        xf, xb = _ffn(xf, ffn1_w_in[i], ffn1_w_out[i], ln1_g[i], ln1_b[i])

        uv = _proj(xb, w_in[i], COL_UV, 2 * BR_WIDTH, F32)
        dqk = _proj(xb, w_in[i], COL_DQK, 2 * BR_WIDTH, BF16, rope=rope)
        dv = _proj(xb, w_in[i], COL_DV, BR_WIDTH, BF16)
        cacg = _proj(xb, w_in[i], COL_CONV, 2 * BR_WIDTH, F32)
        sqkv = _proj(xb, w_in[i], COL_SB, 3 * BR_WIDTH, BF16)

        o_a = _gmlp(uv, gmlp_ln_g[i], gmlp_ln_b[i], gmlp_ws[i], gmlp_bs[i])
        o_b = _diff_attn(dqk, dv, diff_lq1[i], diff_lk1[i], diff_lq2[i], diff_lk2[i], diff_norm_g[i],
                         batch, seq, lam_init)
        o_c = _conv(cacg, conv_w[i], conv_b[i], conv_ln_g[i], conv_ln_b[i], seq)
        o_d = _sb_attn(sqkv, batch, seq)

        merged = _merge(xb, (o_a, o_b, o_c, o_d), w_in[i], w_branch[i])
        xf = _out_ln(merged, w_out[i], xf, ln2_g[i], ln2_b[i])
        xf, _ = _ffn(xf, ffn2_w_in[i], ffn2_w_out[i], ln3_g[i], ln3_b[i])
    return xf.reshape(batch, seq, d)
```

```python
import functools
import math

import jax
import jax.numpy as jnp
from jax import lax
from jax.experimental import pallas as pl
from jax.experimental.pallas import tpu as pltpu

F32 = jnp.float32
BF16 = jnp.bfloat16

D_MODEL = 2048
FF_DIM = 5632
CHUNK = 64
N_BRANCH = 4
BR_WIDTH = 1024
GMLP_BLOCK = 128
GMLP_GROUPS = 4
GMLP_GC = BR_WIDTH // GMLP_GROUPS
DIFF_HEADS = 8
DIFF_D = 64
DIFF_VD = 2 * DIFF_D
SB_HEADS = 8
SB_D = 128
CONV_K = 31
ROT_DIM = DIFF_D // 4
ROPE_THETA = 500000.0
MODEL_DEPTH = 4
ALPHA = (2 * MODEL_DEPTH) ** 0.25
LN_EPS = 1e-5
COL_UV = 0
COL_DQK = 2 * BR_WIDTH
COL_DV = 4 * BR_WIDTH
COL_CONV = 5 * BR_WIDTH
COL_SB = 7 * BR_WIDTH
COL_GATE = 10 * BR_WIDTH

LANES = 128
VMEM_LIMIT_BYTES = 56 * 1024 * 1024

FFN_TM = 512
FFN_TF = 512
PROJ_TM = 1024
PROJ_TN = 1024
PROJ_ROPE_TM = 512
MERGE_TM = 512
MERGE_TN = 1024
OUT_TM = 512
GMLP_TM = 512
CONV_TT = 256
CONV_HALO = 32
ATT_TQ = 512
SB_TK = 256
ROPE_TM = 1024


def _params(*sem):
    return pltpu.CompilerParams(dimension_semantics=sem, vmem_limit_bytes=VMEM_LIMIT_BYTES)


def _sigmoid(x):
    return 1.0 / (1.0 + jnp.exp(-x))


def _chunk_of(size, axis):
    return lax.broadcasted_iota(jnp.int32, (size, size), axis) >> int(math.log2(CHUNK))


def _layer_norm_rows(y, g, b):
    mu = jnp.mean(y, axis=-1, keepdims=True)
    yc = y - mu
    var = jnp.mean(yc * yc, axis=-1, keepdims=True)
    return yc * lax.rsqrt(var + LN_EPS) * g + b


def _ffn_kernel(x_ref, wa_ref, wg_ref, wo_ref, g_ref, b_ref, of_ref, ob_ref, xb_ref, acc_ref):
    j = pl.program_id(1)

    @pl.when(j == 0)
    def _():
        xb_ref[...] = x_ref[...].astype(BF16)

    xb = xb_ref[...]
    a = jnp.dot(xb, wa_ref[...], preferred_element_type=F32)
    gt = jnp.dot(xb, wg_ref[...], preferred_element_type=F32)
    act = (a * _sigmoid(a) * gt).astype(BF16)
    contrib = jnp.dot(act, wo_ref[...], preferred_element_type=F32)

    @pl.when(j == 0)
    def _():
        acc_ref[...] = contrib

    @pl.when(j > 0)
    def _():
        acc_ref[...] += contrib

    @pl.when(j == pl.num_programs(1) - 1)
    def _():
        y = ALPHA * x_ref[...] + 0.5 * acc_ref[...]
        out = _layer_norm_rows(y, g_ref[...], b_ref[...])
        of_ref[...] = out
        ob_ref[...] = out.astype(BF16)


def _ffn(x, w_in, w_out, layer, ln_g, ln_b):
    n, d = x.shape
    f = w_out.shape[1]
    tm = min(FFN_TM, n)
    tf = FFN_TF
    nf = f // tf
    return pl.pallas_call(
        _ffn_kernel,
        grid=(n // tm, nf),
        in_specs=[
            pl.BlockSpec((tm, d), lambda i, j: (i, 0)),
            pl.BlockSpec((None, d, tf), lambda i, j: (layer, 0, j)),
            pl.BlockSpec((None, d, tf), lambda i, j: (layer, 0, j + nf)),
            pl.BlockSpec((None, tf, d), lambda i, j: (layer, j, 0)),
            pl.BlockSpec((1, d), lambda i, j: (0, 0)),
            pl.BlockSpec((1, d), lambda i, j: (0, 0)),
        ],
        out_specs=[
            pl.BlockSpec((tm, d), lambda i, j: (i, 0)),
            pl.BlockSpec((tm, d), lambda i, j: (i, 0)),
        ],
        out_shape=[jax.ShapeDtypeStruct((n, d), F32), jax.ShapeDtypeStruct((n, d), BF16)],
        scratch_shapes=[pltpu.VMEM((tm, d), BF16), pltpu.VMEM((tm, d), F32)],
        compiler_params=_params("parallel", "arbitrary"),
        name="ffn",
    )(x, w_in, w_in, w_out, ln_g.reshape(1, d), ln_b.reshape(1, d))


def _rope_table_kernel(pos_ref, invf_ref, c_ref, s1_ref, s2_ref):
    ang = pos_ref[...] * invf_ref[...]
    cos = jnp.cos(ang)
    sin = jnp.sin(ang)
    lane = lax.broadcasted_iota(jnp.int32, ang.shape, 1) & (DIFF_D - 1)
    half = ROT_DIM // 2
    c_ref[...] = jnp.where(lane < ROT_DIM, cos, 1.0)
    s1_ref[...] = jnp.where(lane < half, -sin, 0.0)
    s2_ref[...] = jnp.where((lane >= half) & (lane < ROT_DIM), sin, 0.0)


def _rope_tables(positions):
    n = positions.size
    tm = min(ROPE_TM, n)
    half = ROT_DIM // 2
    inv_freq = ROPE_THETA ** (-jnp.arange(0, ROT_DIM, 2, dtype=F32) / ROT_DIM)
    per_head = jnp.concatenate([inv_freq, inv_freq, jnp.zeros((DIFF_D - ROT_DIM,), F32)])
    invf = jnp.tile(per_head, LANES // DIFF_D).reshape(1, LANES)
    pos = positions.reshape(n, 1).astype(F32)
    tab = jax.ShapeDtypeStruct((n, LANES), F32)
    return pl.pallas_call(
        _rope_table_kernel,
        grid=(n // tm,),
        in_specs=[pl.BlockSpec((tm, 1), lambda i: (i, 0)), pl.BlockSpec((1, LANES), lambda i: (0, 0))],
        out_specs=[pl.BlockSpec((tm, LANES), lambda i: (i, 0))] * 3,
        out_shape=[tab, tab, tab],
        compiler_params=_params("parallel"),
        name="rope_tables",
    )(pos, invf)


def _proj_kernel(x_ref, w_ref, o_ref):
    o_ref[...] = jnp.dot(x_ref[...], w_ref[...], preferred_element_type=F32).astype(o_ref.dtype)


def _proj_rope_kernel(x_ref, w_ref, c_ref, s1_ref, s2_ref, o_ref, *, q_tiles, q_scale):
    acc = jnp.dot(x_ref[...], w_ref[...], preferred_element_type=F32)
    tn = acc.shape[1]
    reps = tn // LANES
    c = jnp.concatenate([c_ref[...]] * reps, axis=1)
    s1 = jnp.concatenate([s1_ref[...]] * reps, axis=1)
    s2 = jnp.concatenate([s2_ref[...]] * reps, axis=1)
    half = ROT_DIM // 2
    nxt = pltpu.roll(acc, tn - half, axis=1)
    prv = pltpu.roll(acc, half, axis=1)
    out = acc * c + nxt * s1 + prv * s2
    scale = jnp.where(pl.program_id(1) < q_tiles, q_scale, 1.0).astype(F32)
    o_ref[...] = (out * scale).astype(o_ref.dtype)


def _proj(xb, w, layer, col0, ncols, out_dtype, rope=None):
    n, d = xb.shape
    tm = min(PROJ_TM if rope is None else PROJ_ROPE_TM, n)
    tn = PROJ_TN
    off = col0 // tn
    x_spec = pl.BlockSpec((tm, d), lambda i, j: (i, 0))
    w_spec = pl.BlockSpec((None, d, tn), lambda i, j: (layer, 0, j + off))
    o_spec = pl.BlockSpec((tm, tn), lambda i, j: (i, j))
    common = dict(
        grid=(n // tm, ncols // tn),
        out_specs=o_spec,
        out_shape=jax.ShapeDtypeStruct((n, ncols), out_dtype),
        compiler_params=_params("parallel", "arbitrary"),
    )
    if rope is None:
        return pl.pallas_call(_proj_kernel, in_specs=[x_spec, w_spec], name="proj", **common)(xb, w)
    t_spec = pl.BlockSpec((tm, LANES), lambda i, j: (i, 0))
    kern = functools.partial(_proj_rope_kernel, q_tiles=(ncols // 2) // tn, q_scale=DIFF_D ** -0.5)
    return pl.pallas_call(kern, in_specs=[x_spec, w_spec, t_spec, t_spec, t_spec], name="proj_rope",
                          **common)(xb, w, *rope)


def _gmlp_kernel(u_ref, v_ref, g_ref, b_ref, ws_ref, bs_ref, o_ref):
    tm = u_ref.shape[0]
    visible = _chunk_of(GMLP_BLOCK, 1) <= _chunk_of(GMLP_BLOCK, 0)
    g = g_ref[...]
    b = b_ref[...]
    for blk in range(tm // GMLP_BLOCK):
        rows = pl.ds(blk * GMLP_BLOCK, GMLP_BLOCK)
        vn = _layer_norm_rows(v_ref[rows, :], g, b).astype(BF16)
        for grp in range(GMLP_GROUPS):
            cols = pl.ds(grp * GMLP_GC, GMLP_GC)
            w = jnp.where(visible, ws_ref[grp], 0.0).astype(BF16)
            mixed = jnp.dot(w, vn[:, grp * GMLP_GC:(grp + 1) * GMLP_GC], preferred_element_type=F32)
            mixed = mixed + bs_ref[:, grp:grp + 1]
            o_ref[rows, cols] = (u_ref[rows, cols] * mixed).astype(o_ref.dtype)


def _gmlp(uv, ln_g, ln_b, ws, bs):
    n = uv.shape[0]
    tm = min(GMLP_TM, n)
    return pl.pallas_call(
        _gmlp_kernel,
        grid=(n // tm,),
        in_specs=[
            pl.BlockSpec((tm, BR_WIDTH), lambda i: (i, 0)),
            pl.BlockSpec((tm, BR_WIDTH), lambda i: (i, 1)),
            pl.BlockSpec((1, BR_WIDTH), lambda i: (0, 0)),
            pl.BlockSpec((1, BR_WIDTH), lambda i: (0, 0)),
            pl.BlockSpec((GMLP_GROUPS, GMLP_BLOCK, GMLP_BLOCK), lambda i: (0, 0, 0)),
            pl.BlockSpec((GMLP_BLOCK, GMLP_GROUPS), lambda i: (0, 0)),
        ],
        out_specs=pl.BlockSpec((tm, BR_WIDTH), lambda i: (i, 0)),
        out_shape=jax.ShapeDtypeStruct((n, BR_WIDTH), BF16),
        compiler_params=_params("parallel"),
        name="gmlp",
    )(uv, uv, ln_g.reshape(1, -1), ln_b.reshape(1, -1), ws, bs.T)


def _store_transposed_blocks(v_ref, vt_ref):
    nblk, _, tk = vt_ref.shape
    for c in range(nblk):
        blk = v_ref[pl.ds(c * tk, tk), :].astype(F32)
        vt_ref[c] = blk.T.astype(vt_ref.dtype)


def _diff_attn_kernel(q_ref, k_ref, v_ref, lq1_ref, lk1_ref, lq2_ref, lk2_ref, ng_ref, o_ref,
                      vt_ref, s0_ref, s1_ref, bmax0_ref, bmax1_ref, m_ref, l_ref, acc_ref, *, lam_init):
    qi = pl.program_id(2)
    tq = q_ref.shape[0]
    q = q_ref[...]
    lane = lax.broadcasted_iota(jnp.int32, q.shape, 1)
    q_both = jnp.concatenate([jnp.where(lane < DIFF_D, q, 0), jnp.where(lane >= DIFF_D, q, 0)], axis=0)

    @pl.when(qi == 0)
    def _():
        _store_transposed_blocks(v_ref, vt_ref)

    m_ref[...] = jnp.full(m_ref.shape, -1e30, F32)
    l_ref[...] = jnp.zeros(l_ref.shape, F32)
    acc_ref[...] = jnp.zeros(acc_ref.shape, F32)

    s_slots = ((s0_ref, bmax0_ref), (s1_ref, bmax1_ref))

    def scores(j, slot):
        s_ref, bmax_ref = s_slots[slot]
        kb = k_ref[pl.ds(pl.multiple_of(j * tq, tq), tq), :]
        s = lax.dot_general(kb, q_both, (((1,), (1,)), ((), ())), preferred_element_type=F32)
        s_ref[...] = s
        bmax_ref[...] = jnp.max(s, axis=0, keepdims=True)

    def softmax_pv(j, slot, visible):
        s_ref, bmax_ref = s_slots[slot]
        s = s_ref[...]
        if visible is None:
            block_max = bmax_ref[...]
        else:
            s = jnp.where(visible, s, -jnp.inf)
            block_max = jnp.max(s, axis=0, keepdims=True)
        m_old = m_ref[...]
        m_new = jnp.maximum(m_old, block_max)
        p = jnp.exp(s - m_new)
        corr = jnp.exp(m_old - m_new)
        l_ref[...] = corr * l_ref[...] + jnp.sum(p, axis=0, keepdims=True)
        acc_ref[...] = corr * acc_ref[...] + jnp.dot(vt_ref[j], p.astype(BF16), preferred_element_type=F32)
        m_ref[...] = m_new

    scores(0, 0)

    def body(j, carry):
        for slot in range(2):
            @pl.when((j & 1) == slot)
            def _(slot=slot):
                softmax_pv(j, slot, None)
                scores(j + 1, 1 - slot)
        return carry

    lax.fori_loop(0, qi, body, 0)
    visible = _chunk_of(tq, 0) <= _chunk_of(tq, 1)
    visible = jnp.concatenate([visible, visible], axis=1)
    for slot in range(2):
        @pl.when((qi & 1) == slot)
        def _(slot=slot):
            softmax_pv(qi, slot, visible)

    lam = (jnp.exp(jnp.sum(lq1_ref[...] * lk1_ref[...], axis=-1, keepdims=True))
           - jnp.exp(jnp.sum(lq2_ref[...] * lk2_ref[...], axis=-1, keepdims=True)) + lam_init)
    o_both = acc_ref[...] * (1.0 / l_ref[...])
    ot = o_both[:, :tq] - lam * o_both[:, tq:]
    ot = ot * lax.rsqrt(jnp.mean(ot * ot, axis=0, keepdims=True) + LN_EPS)
    o_ref[...] = (ot.T * ng_ref[...] * (1.0 - lam_init)).astype(o_ref.dtype)


def _diff_attn(dqk, dv, lq1, lk1, lq2, lk2, norm_g, batch, seq, lam_init):
    n = dqk.shape[0]
    tq = min(ATT_TQ, seq)
    nq = seq // tq
    vec = pl.BlockSpec((1, DIFF_D), lambda b, h, i: (0, 0))
    return pl.pallas_call(
        functools.partial(_diff_attn_kernel, lam_init=lam_init),
        grid=(batch, DIFF_HEADS, nq),
        in_specs=[
            pl.BlockSpec((tq, DIFF_VD), lambda b, h, i: (b * nq + i, h)),
            pl.BlockSpec((seq, DIFF_VD), lambda b, h, i: (b, DIFF_HEADS + h)),
            pl.BlockSpec((seq, DIFF_VD), lambda b, h, i: (b, h)),
            vec, vec, vec, vec,
            pl.BlockSpec((1, DIFF_VD), lambda b, h, i: (0, h)),
        ],
        out_specs=pl.BlockSpec((tq, DIFF_VD), lambda b, h, i: (b * nq + i, h)),
        out_shape=jax.ShapeDtypeStruct((n, DIFF_HEADS * DIFF_VD), BF16),
        scratch_shapes=[pltpu.VMEM((nq, DIFF_VD, tq), BF16),
                        pltpu.VMEM((tq, 2 * tq), F32), pltpu.VMEM((tq, 2 * tq), F32),
                        pltpu.VMEM((1, 2 * tq), F32), pltpu.VMEM((1, 2 * tq), F32),
                        pltpu.VMEM((1, 2 * tq), F32), pltpu.VMEM((1, 2 * tq), F32),
                        pltpu.VMEM((DIFF_VD, 2 * tq), F32)],
        compiler_params=_params("parallel", "parallel", "arbitrary"),
        name="diff_attn",
    )(dqk, dqk, dv, lq1.reshape(1, -1), lk1.reshape(1, -1), lq2.reshape(1, -1), lk2.reshape(1, -1),
      norm_g.reshape(1, -1))


def _conv_kernel(a_ref, g_ref, ah_ref, gh_ref, w_ref, cb_ref, lg_ref, lb_ref, o_ref, h_ref, y_ref,
                 *, tiles_per_seq):
    tt = a_ref.shape[0]
    first = (pl.program_id(0) % tiles_per_seq) == 0
    halo = ah_ref[...] * _sigmoid(gh_ref[...])
    h_ref[pl.ds(0, CONV_HALO), :] = jnp.where(first, 0.0, halo)
    h_ref[pl.ds(CONV_HALO, tt), :] = a_ref[...] * _sigmoid(g_ref[...])
    base = CONV_HALO - (CONV_K - 1)
    for cb in range(a_ref.shape[1] // LANES):
        cols = pl.ds(cb * LANES, LANES)
        acc = jnp.zeros((tt, LANES), F32)
        for r in range(8):
            taps = [k for k in range(CONV_K) if k % 8 == r]
            shifted = h_ref[pl.ds(base + r, tt + taps[-1] - r), cols]
            for k in taps:
                acc = acc + w_ref[pl.ds(k, 1), cols] * shifted[k - r:k - r + tt, :]
        y_ref[:, cols] = acc + cb_ref[:, cols]
    y = _layer_norm_rows(y_ref[...], lg_ref[...], lb_ref[...])
    o_ref[...] = (y * _sigmoid(y)).astype(o_ref.dtype)


def _conv(cacg, w_dw, b_dw, ln_g, ln_b, seq):
    n = cacg.shape[0]
    c = BR_WIDTH
    tt = min(CONV_TT, seq)
    ratio = tt // CONV_HALO
    row = lambda v: v.reshape(1, c)
    return pl.pallas_call(
        functools.partial(_conv_kernel, tiles_per_seq=seq // tt),
        grid=(n // tt,),
        in_specs=[
            pl.BlockSpec((tt, c), lambda i: (i, 0)),
            pl.BlockSpec((tt, c), lambda i: (i, 1)),
            pl.BlockSpec((CONV_HALO, c), lambda i: (jnp.maximum(i * ratio - 1, 0), 0)),
            pl.BlockSpec((CONV_HALO, c), lambda i: (jnp.maximum(i * ratio - 1, 0), 1)),
            pl.BlockSpec((CONV_K, c), lambda i: (0, 0)),
            pl.BlockSpec((1, c), lambda i: (0, 0)),
            pl.BlockSpec((1, c), lambda i: (0, 0)),
            pl.BlockSpec((1, c), lambda i: (0, 0)),
        ],
        out_specs=pl.BlockSpec((tt, c), lambda i: (i, 0)),
        out_shape=jax.ShapeDtypeStruct((n, c), BF16),
        scratch_shapes=[pltpu.VMEM((tt + CONV_HALO, c), F32), pltpu.VMEM((tt, c), F32)],
        compiler_params=_params("parallel"),
        name="conformer_conv",
    )(cacg, cacg, cacg, cacg, w_dw, row(b_dw), row(ln_g), row(ln_b))


def _sb_attn_kernel(q_ref, k_ref, v_ref, o_ref, vt_ref, logw0_ref, logw1_ref, tot0_ref, tot1_ref,
                    rest_ref, acc_ref, *, tk):
    qi = pl.program_id(2)
    tq = q_ref.shape[0]
    q = q_ref[...]

    @pl.when(qi == 0)
    def _():
        _store_transposed_blocks(v_ref, vt_ref)

    to_log2 = SB_D ** -0.5 * math.log2(math.e)
    key_a = lax.broadcasted_iota(jnp.int32, (tk, tk), 0)
    key_b = lax.broadcasted_iota(jnp.int32, (tk, tk), 1)
    later = jnp.where(key_b > key_a, 1.0, 0.0).astype(BF16)
    suffix_ones = jnp.concatenate([later, later], axis=1)

    rest_ref[...] = jnp.zeros(rest_ref.shape, F32)
    acc_ref[...] = jnp.zeros(acc_ref.shape, F32)
    slots = ((logw0_ref, tot0_ref), (logw1_ref, tot1_ref))

    def prepare(j, slot, diagonal):
        logw_ref, tot_ref = slots[slot]
        kb = k_ref[pl.ds(pl.multiple_of(j * tq, tq), tq), :]
        z = lax.dot_general(kb, q, (((1,), (1,)), ((), ())), preferred_element_type=F32) * to_log2
        log_rest = -(jnp.maximum(z, 0.0) + jnp.log2(1.0 + jnp.exp2(-jnp.abs(z))))
        if diagonal:
            mask = (lax.broadcasted_iota(jnp.int32, (tq, tq), 0)
                    < lax.broadcasted_iota(jnp.int32, (tq, tq), 1))
            log_rest = jnp.where(mask, log_rest, 0.0)
        hi = log_rest.astype(BF16)
        lo = (log_rest - hi.astype(F32)).astype(BF16)
        running = jnp.zeros((1, tq), F32)
        for sub in reversed(range(tq // tk)):
            rows = slice(sub * tk, (sub + 1) * tk)
            suffix = jnp.dot(suffix_ones, jnp.concatenate([hi[rows], lo[rows]], axis=0),
                             preferred_element_type=F32)
            logw = (z[rows] + log_rest[rows]) + suffix + running
            if diagonal:
                logw = jnp.where(mask[rows], logw, -1e30)
            logw_ref[rows, :] = logw
            running = running + jnp.sum(log_rest[rows], axis=0, keepdims=True)
        tot_ref[...] = running

    def accumulate(j, slot):
        logw_ref, tot_ref = slots[slot]
        rest = rest_ref[...]
        w = jnp.exp2(logw_ref[...] + rest)
        acc_ref[...] += jnp.dot(vt_ref[j], w.astype(BF16), preferred_element_type=F32)
        rest_ref[...] = rest + tot_ref[...]

    prepare(qi, 0, True)

    def body(jj, carry):
        for slot in range(2):
            @pl.when((jj & 1) == slot)
            def _(slot=slot):
                accumulate(qi - jj, slot)
                prepare(qi - 1 - jj, 1 - slot, False)
        return carry

    lax.fori_loop(0, qi, body, 0)
    for slot in range(2):
        @pl.when((qi & 1) == slot)
        def _(slot=slot):
            accumulate(0, slot)
    o_ref[...] = acc_ref[...].T.astype(o_ref.dtype)


def _sb_attn(sqkv, batch, seq):
    n = sqkv.shape[0]
    tq = min(ATT_TQ, seq)
    nq = seq // tq
    tk = min(SB_TK, tq)
    return pl.pallas_call(
        functools.partial(_sb_attn_kernel, tk=tk),
        grid=(batch, SB_HEADS, nq),
        in_specs=[
            pl.BlockSpec((tq, SB_D), lambda b, h, i: (b * nq + i, h)),
            pl.BlockSpec((seq, SB_D), lambda b, h, i: (b, SB_HEADS + h)),
            pl.BlockSpec((seq, SB_D), lambda b, h, i: (b, 2 * SB_HEADS + h)),
        ],
        out_specs=pl.BlockSpec((tq, SB_D), lambda b, h, i: (b * nq + i, h)),
        out_shape=jax.ShapeDtypeStruct((n, SB_HEADS * SB_D), BF16),
        scratch_shapes=[pltpu.VMEM((nq, SB_D, tq), BF16),
                        pltpu.VMEM((tq, tq), F32), pltpu.VMEM((tq, tq), F32),
                        pltpu.VMEM((1, tq), F32), pltpu.VMEM((1, tq), F32),
                        pltpu.VMEM((1, tq), F32), pltpu.VMEM((SB_D, tq), F32)],
        compiler_params=_params("parallel", "parallel", "arbitrary"),
        name="sb_attn",
    )(sqkv, sqkv, sqkv)


def _merge_kernel(h_ref, oa_ref, ob_ref, oc_ref, od_ref, wg_ref, wb_ref, o_ref, acc_ref):
    b = pl.program_id(2)
    gate = _sigmoid(jnp.dot(h_ref[...], wg_ref[...], preferred_element_type=F32))
    for idx, br_ref in enumerate((oa_ref, ob_ref, oc_ref, od_ref)):
        @pl.when(b == idx)
        def _(br_ref=br_ref, first=(idx == 0)):
            term = gate * jnp.dot(br_ref[...], wb_ref[...], preferred_element_type=F32)
            if first:
                acc_ref[...] = term
            else:
                acc_ref[...] += term

    @pl.when(b == N_BRANCH - 1)
    def _():
        o_ref[...] = acc_ref[...].astype(o_ref.dtype)


def _merge(hb, branches, w_in, w_branch, layer):
    n, d = hb.shape
    tm = min(MERGE_TM, n)
    tn = MERGE_TN
    nc = d // tn
    gate_off = COL_GATE // tn
    br_spec = pl.BlockSpec((tm, BR_WIDTH), lambda i, c, b: (i, 0))
    return pl.pallas_call(
        _merge_kernel,
        grid=(n // tm, nc, N_BRANCH),
        in_specs=[
            pl.BlockSpec((tm, d), lambda i, c, b: (i, 0)),
            br_spec, br_spec, br_spec, br_spec,
            pl.BlockSpec((None, d, tn), lambda i, c, b: (layer, 0, gate_off + b * nc + c)),
            pl.BlockSpec((None, None, BR_WIDTH, tn), lambda i, c, b: (layer, b, 0, c)),
        ],
        out_specs=pl.BlockSpec((tm, tn), lambda i, c, b: (i, c)),
        out_shape=jax.ShapeDtypeStruct((n, d), BF16),
        scratch_shapes=[pltpu.VMEM((tm, tn), F32)],
        compiler_params=_params("parallel", "arbitrary", "arbitrary"),
        name="merge",
    )(hb, *branches, w_in, w_branch)


def _out_ln_kernel(m_ref, w_ref, x_ref, g_ref, b_ref, o_ref):
    y = ALPHA * x_ref[...] + jnp.dot(m_ref[...], w_ref[...], preferred_element_type=F32)
    o_ref[...] = _layer_norm_rows(y, g_ref[...], b_ref[...])


def _out_ln(merged, w_out, layer, x, ln_g, ln_b):
    n, d = x.shape
    tm = min(OUT_TM, n)
    return pl.pallas_call(
        _out_ln_kernel,
        grid=(n // tm,),
        in_specs=[
            pl.BlockSpec((tm, d), lambda i: (i, 0)),
            pl.BlockSpec((None, d, d), lambda i: (layer, 0, 0)),
            pl.BlockSpec((tm, d), lambda i: (i, 0)),
            pl.BlockSpec((1, d), lambda i: (0, 0)),
            pl.BlockSpec((1, d), lambda i: (0, 0)),
        ],
        out_specs=pl.BlockSpec((tm, d), lambda i: (i, 0)),
        out_shape=jax.ShapeDtypeStruct((n, d), F32),
        compiler_params=_params("parallel"),
        name="out_proj_ln",
    )(merged, w_out, x, ln_g.reshape(1, d), ln_b.reshape(1, d))


def kernel(x, positions, ffn1_w_in, ffn1_w_out, ln1_g, ln1_b, w_in, gmlp_ln_g, gmlp_ln_b, gmlp_ws, gmlp_bs,
           diff_lq1, diff_lk1, diff_lq2, diff_lk2, diff_norm_g, conv_w, conv_b, conv_ln_g, conv_ln_b,
           w_branch, w_out, ln2_g, ln2_b, ffn2_w_in, ffn2_w_out, ln3_g, ln3_b):
    batch, seq, d = x.shape
    n = batch * seq
    depth = w_in.shape[0]
    ffn1_w_in, ffn1_w_out, ffn2_w_in, ffn2_w_out, w_in, w_branch, w_out = (
        w.astype(BF16) for w in (ffn1_w_in, ffn1_w_out, ffn2_w_in, ffn2_w_out, w_in, w_branch, w_out))

    rope = _rope_tables(positions)
    xf = x.reshape(n, d)
    for i in range(depth):
        lam_init = 0.8 - 0.6 * math.exp(-0.3 * i)
        xf, xb = _ffn(xf, ffn1_w_in, ffn1_w_out, i, ln1_g[i], ln1_b[i])

        uv = _proj(xb, w_in, i, COL_UV, 2 * BR_WIDTH, F32)
        dqk = _proj(xb, w_in, i, COL_DQK, 2 * BR_WIDTH, BF16, rope=rope)
        dv = _proj(xb, w_in, i, COL_DV, BR_WIDTH, BF16)
        cacg = _proj(xb, w_in, i, COL_CONV, 2 * BR_WIDTH, F32)
        sqkv = _proj(xb, w_in, i, COL_SB, 3 * BR_WIDTH, BF16)

        o_a = _gmlp(uv, gmlp_ln_g[i], gmlp_ln_b[i], gmlp_ws[i], gmlp_bs[i])
        o_b = _diff_attn(dqk, dv, diff_lq1[i], diff_lk1[i], diff_lq2[i], diff_lk2[i], diff_norm_g[i],
                         batch, seq, lam_init)
        o_c = _conv(cacg, conv_w[i], conv_b[i], conv_ln_g[i], conv_ln_b[i], seq)
        o_d = _sb_attn(sqkv, batch, seq)

        merged = _merge(xb, (o_a, o_b, o_c, o_d), w_in, w_branch, i)
        xf = _out_ln(merged, w_out, i, xf, ln2_g[i], ln2_b[i])
        xf, _ = _ffn(xf, ffn2_w_in, ffn2_w_out, i, ln3_g[i], ln3_b[i])
    return xf.reshape(batch, seq, d)
```

```python
import functools
import math

import jax
import jax.numpy as jnp
from jax import lax
from jax.experimental import pallas as pl
from jax.experimental.pallas import tpu as pltpu

F32 = jnp.float32
BF16 = jnp.bfloat16

D_MODEL = 2048
FF_DIM = 5632
CHUNK = 64
N_BRANCH = 4
BR_WIDTH = 1024
GMLP_BLOCK = 128
GMLP_GROUPS = 4
GMLP_GC = BR_WIDTH // GMLP_GROUPS
DIFF_HEADS = 8
DIFF_D = 64
DIFF_VD = 2 * DIFF_D
SB_HEADS = 8
SB_D = 128
CONV_K = 31
ROT_DIM = DIFF_D // 4
ROPE_THETA = 500000.0
MODEL_DEPTH = 4
ALPHA = (2 * MODEL_DEPTH) ** 0.25
LN_EPS = 1e-5
COL_UV = 0
COL_DQK = 2 * BR_WIDTH
COL_DV = 4 * BR_WIDTH
COL_CONV = 5 * BR_WIDTH
COL_SB = 7 * BR_WIDTH
COL_GATE = 10 * BR_WIDTH

LANES = 128
F32_SUBLANES = 8
BF16_SUBLANES = 16
VMEM_LIMIT_BYTES = 56 * 1024 * 1024

FFN_TM = 512
FFN_TF = 512
PROJ_TM = 1024
PROJ_TN = 1024
PROJ_ROPE_TM = 512
MERGE_TM = 512
MERGE_TN = 512
OUT_TM = 512
GMLP_TM = 512
CONV_TT = 256
CONV_HALO = 32
ATT_TQ = 512
SB_TK = 256
ROPE_TM = 1024


def _params(*sem):
    return pltpu.CompilerParams(dimension_semantics=sem, vmem_limit_bytes=VMEM_LIMIT_BYTES)


def _sigmoid(x):
    return 1.0 / (1.0 + jnp.exp(-x))


def _chunk_of(size, axis):
    return lax.broadcasted_iota(jnp.int32, (size, size), axis) >> int(math.log2(CHUNK))


def _layer_norm_rows(y, g, b):
    mu = jnp.mean(y, axis=-1, keepdims=True)
    yc = y - mu
    var = jnp.mean(yc * yc, axis=-1, keepdims=True)
    return yc * lax.rsqrt(var + LN_EPS) * g + b


def _ffn_kernel(x_ref, wa_ref, wg_ref, wo_ref, g_ref, b_ref, of_ref, ob_ref, xb_ref, acc_ref,
                act0_ref, act1_ref, *, nf):
    j = pl.program_id(1)
    acts = (act0_ref, act1_ref)

    def activate(dst_ref):
        xb = xb_ref[...]
        a = jnp.dot(xb, wa_ref[...], preferred_element_type=F32)
        gt = jnp.dot(xb, wg_ref[...], preferred_element_type=F32)
        dst_ref[...] = (a * _sigmoid(a) * gt).astype(BF16)

    def project_down(src_ref):
        acc_ref[...] += jnp.dot(src_ref[...], wo_ref[...], preferred_element_type=F32)

    @pl.when(j == 0)
    def _():
        xb_ref[...] = x_ref[...].astype(BF16)
        acc_ref[...] = jnp.zeros(acc_ref.shape, F32)
        activate(acts[0])

    for slot in range(2):
        @pl.when((j > 0) & (j < nf) & ((j & 1) == slot))
        def _(slot=slot):
            project_down(acts[1 - slot])
            activate(acts[slot])

    @pl.when(j == nf)
    def _():
        project_down(acts[(nf - 1) % 2])
        y = ALPHA * x_ref[...] + 0.5 * acc_ref[...]
        out = _layer_norm_rows(y, g_ref[...], b_ref[...])
        of_ref[...] = out
        ob_ref[...] = out.astype(BF16)


def _ffn(x, w_in, w_out, layer, ln_g, ln_b):
    n, d = x.shape
    f = w_out.shape[1]
    tm = min(FFN_TM, n)
    tf = FFN_TF
    nf = f // tf
    last = nf - 1
    return pl.pallas_call(
        functools.partial(_ffn_kernel, nf=nf),
        grid=(n // tm, nf + 1),
        in_specs=[
            pl.BlockSpec((tm, d), lambda i, j: (i, 0)),
            pl.BlockSpec((None, d, tf), lambda i, j: (layer, 0, jnp.minimum(j, last))),
            pl.BlockSpec((None, d, tf), lambda i, j: (layer, 0, jnp.minimum(j, last) + nf)),
            pl.BlockSpec((None, tf, d), lambda i, j: (layer, jnp.maximum(j - 1, 0), 0)),
            pl.BlockSpec((1, d), lambda i, j: (0, 0)),
            pl.BlockSpec((1, d), lambda i, j: (0, 0)),
        ],
        out_specs=[
            pl.BlockSpec((tm, d), lambda i, j: (i, 0)),
            pl.BlockSpec((tm, d), lambda i, j: (i, 0)),
        ],
        out_shape=[jax.ShapeDtypeStruct((n, d), F32), jax.ShapeDtypeStruct((n, d), BF16)],
        scratch_shapes=[pltpu.VMEM((tm, d), BF16), pltpu.VMEM((tm, d), F32),
                        pltpu.VMEM((tm, tf), BF16), pltpu.VMEM((tm, tf), BF16)],
        compiler_params=_params("parallel", "arbitrary"),
        name="ffn",
    )(x, w_in, w_in, w_out, ln_g.reshape(1, d), ln_b.reshape(1, d))


def _rope_table_kernel(pos_ref, invf_ref, c_ref, s1_ref, s2_ref):
    ang = pos_ref[...] * invf_ref[...]
    cos = jnp.cos(ang)
    sin = jnp.sin(ang)
    lane = lax.broadcasted_iota(jnp.int32, ang.shape, 1) & (DIFF_D - 1)
    half = ROT_DIM // 2
    c_ref[...] = jnp.where(lane < ROT_DIM, cos, 1.0)
    s1_ref[...] = jnp.where(lane < half, -sin, 0.0)
    s2_ref[...] = jnp.where((lane >= half) & (lane < ROT_DIM), sin, 0.0)


def _rope_tables(positions):
    n = positions.size
    tm = min(ROPE_TM, n)
    half = ROT_DIM // 2
    inv_freq = ROPE_THETA ** (-jnp.arange(0, ROT_DIM, 2, dtype=F32) / ROT_DIM)
    per_head = jnp.concatenate([inv_freq, inv_freq, jnp.zeros((DIFF_D - ROT_DIM,), F32)])
    invf = jnp.tile(per_head, LANES // DIFF_D).reshape(1, LANES)
    pos = positions.reshape(n, 1).astype(F32)
    tab = jax.ShapeDtypeStruct((n, LANES), F32)
    return pl.pallas_call(
        _rope_table_kernel,
        grid=(n // tm,),
        in_specs=[pl.BlockSpec((tm, 1), lambda i: (i, 0)), pl.BlockSpec((1, LANES), lambda i: (0, 0))],
        out_specs=[pl.BlockSpec((tm, LANES), lambda i: (i, 0))] * 3,
        out_shape=[tab, tab, tab],
        compiler_params=_params("parallel"),
        name="rope_tables",
    )(pos, invf)


def _proj_kernel(x_ref, w_ref, o_ref):
    o_ref[...] = jnp.dot(x_ref[...], w_ref[...], preferred_element_type=F32).astype(o_ref.dtype)


def _proj_rope_kernel(x_ref, w_ref, c_ref, s1_ref, s2_ref, o_ref, *, q_tiles, q_scale):
    acc = jnp.dot(x_ref[...], w_ref[...], preferred_element_type=F32)
    tn = acc.shape[1]
    reps = tn // LANES
    c = jnp.concatenate([c_ref[...]] * reps, axis=1)
    s1 = jnp.concatenate([s1_ref[...]] * reps, axis=1)
    s2 = jnp.concatenate([s2_ref[...]] * reps, axis=1)
    half = ROT_DIM // 2
    nxt = pltpu.roll(acc, tn - half, axis=1)
    prv = pltpu.roll(acc, half, axis=1)
    out = acc * c + nxt * s1 + prv * s2
    scale = jnp.where(pl.program_id(1) < q_tiles, q_scale, 1.0).astype(F32)
    o_ref[...] = (out * scale).astype(o_ref.dtype)


def _proj(xb, w, layer, col0, ncols, out_dtype, rope=None):
    n, d = xb.shape
    tm = min(PROJ_TM if rope is None else PROJ_ROPE_TM, n)
    tn = PROJ_TN
    off = col0 // tn
    x_spec = pl.BlockSpec((tm, d), lambda i, j: (i, 0))
    w_spec = pl.BlockSpec((None, d, tn), lambda i, j: (layer, 0, j + off))
    o_spec = pl.BlockSpec((tm, tn), lambda i, j: (i, j))
    common = dict(
        grid=(n // tm, ncols // tn),
        out_specs=o_spec,
        out_shape=jax.ShapeDtypeStruct((n, ncols), out_dtype),
        compiler_params=_params("parallel", "arbitrary"),
    )
    if rope is None:
        return pl.pallas_call(_proj_kernel, in_specs=[x_spec, w_spec], name="proj", **common)(xb, w)
    t_spec = pl.BlockSpec((tm, LANES), lambda i, j: (i, 0))
    kern = functools.partial(_proj_rope_kernel, q_tiles=(ncols // 2) // tn, q_scale=DIFF_D ** -0.5)
    return pl.pallas_call(kern, in_specs=[x_spec, w_spec, t_spec, t_spec, t_spec], name="proj_rope",
                          **common)(xb, w, *rope)


def _gmlp_kernel(u_ref, v_ref, g_ref, b_ref, ws_ref, bs_ref, o_ref):
    tm = u_ref.shape[0]
    visible = _chunk_of(GMLP_BLOCK, 1) <= _chunk_of(GMLP_BLOCK, 0)
    g = g_ref[...]
    b = b_ref[...]
    for blk in range(tm // GMLP_BLOCK):
        rows = pl.ds(blk * GMLP_BLOCK, GMLP_BLOCK)
        vn = _layer_norm_rows(v_ref[rows, :], g, b).astype(BF16)
        for grp in range(GMLP_GROUPS):
            cols = pl.ds(grp * GMLP_GC, GMLP_GC)
            w = jnp.where(visible, ws_ref[grp], 0.0).astype(BF16)
            mixed = jnp.dot(w, vn[:, grp * GMLP_GC:(grp + 1) * GMLP_GC], preferred_element_type=F32)
            mixed = mixed + bs_ref[:, grp:grp + 1]
            o_ref[rows, cols] = (u_ref[rows, cols] * mixed).astype(o_ref.dtype)


def _gmlp(uv, ln_g, ln_b, ws, bs):
    n = uv.shape[0]
    tm = min(GMLP_TM, n)
    return pl.pallas_call(
        _gmlp_kernel,
        grid=(n // tm,),
        in_specs=[
            pl.BlockSpec((tm, BR_WIDTH), lambda i: (i, 0)),
            pl.BlockSpec((tm, BR_WIDTH), lambda i: (i, 1)),
            pl.BlockSpec((1, BR_WIDTH), lambda i: (0, 0)),
            pl.BlockSpec((1, BR_WIDTH), lambda i: (0, 0)),
            pl.BlockSpec((GMLP_GROUPS, GMLP_BLOCK, GMLP_BLOCK), lambda i: (0, 0, 0)),
            pl.BlockSpec((GMLP_BLOCK, GMLP_GROUPS), lambda i: (0, 0)),
        ],
        out_specs=pl.BlockSpec((tm, BR_WIDTH), lambda i: (i, 0)),
        out_shape=jax.ShapeDtypeStruct((n, BR_WIDTH), BF16),
        compiler_params=_params("parallel"),
        name="gmlp",
    )(uv, uv, ln_g.reshape(1, -1), ln_b.reshape(1, -1), ws, bs.T)


def _store_transposed_blocks(v_ref, vt_ref):
    nblk, rows, tk = vt_ref.shape
    d = v_ref.shape[1]
    for c in range(nblk):
        blk = v_ref[pl.ds(c * tk, tk), :].astype(F32)
        vt_ref[c, pl.ds(0, d), :] = blk.T.astype(vt_ref.dtype)
        if rows > d:
            first = lax.broadcasted_iota(jnp.int32, (rows - d, tk), 0) == 0
            vt_ref[c, pl.ds(d, rows - d), :] = jnp.where(first, 1.0, 0.0).astype(vt_ref.dtype)


def _diff_attn_kernel(q_ref, k_ref, v_ref, lq1_ref, lk1_ref, lq2_ref, lk2_ref, ng_ref, o_ref,
                      vt_ref, s0_ref, s1_ref, bmax0_ref, bmax1_ref, m_ref, acc_ref, *, lam_init):
    qi = pl.program_id(2)
    tq = q_ref.shape[0]
    q = q_ref[...]
    lane = lax.broadcasted_iota(jnp.int32, q.shape, 1)
    q_both = jnp.concatenate([jnp.where(lane < DIFF_D, q, 0), jnp.where(lane >= DIFF_D, q, 0)], axis=0)

    @pl.when(qi == 0)
    def _():
        _store_transposed_blocks(v_ref, vt_ref)

    m_ref[...] = jnp.full(m_ref.shape, -1e30, F32)
    acc_ref[...] = jnp.zeros(acc_ref.shape, F32)

    s_slots = ((s0_ref, bmax0_ref), (s1_ref, bmax1_ref))

    def scores(j, slot):
        s_ref, bmax_ref = s_slots[slot]
        kb = k_ref[pl.ds(pl.multiple_of(j * tq, tq), tq), :]
        s = lax.dot_general(kb, q_both, (((1,), (1,)), ((), ())), preferred_element_type=F32)
        s_ref[...] = s
        bmax_ref[...] = jnp.max(s, axis=0, keepdims=True)

    def softmax_pv(j, slot, visible):
        s_ref, bmax_ref = s_slots[slot]
        s = s_ref[...]
        if visible is None:
            block_max = bmax_ref[...]
        else:
            s = jnp.where(visible, s, -jnp.inf)
            block_max = jnp.max(s, axis=0, keepdims=True)
        m_old = m_ref[...]
        m_new = jnp.maximum(m_old, block_max)
        p = jnp.exp(s - m_new)
        corr = jnp.exp(m_old - m_new)
        acc_ref[...] = corr * acc_ref[...] + jnp.dot(vt_ref[j], p.astype(BF16), preferred_element_type=F32)
        m_ref[...] = m_new

    scores(0, 0)

    def body(j, carry):
        for slot in range(2):
            @pl.when((j & 1) == slot)
            def _(slot=slot):
                scores(j + 1, 1 - slot)
                softmax_pv(j, slot, None)
        return carry

    lax.fori_loop(0, qi, body, 0)
    visible = _chunk_of(tq, 0) <= _chunk_of(tq, 1)
    visible = jnp.concatenate([visible, visible], axis=1)
    for slot in range(2):
        @pl.when((qi & 1) == slot)
        def _(slot=slot):
            softmax_pv(qi, slot, visible)

    lam = (jnp.exp(jnp.sum(lq1_ref[...] * lk1_ref[...], axis=-1, keepdims=True))
           - jnp.exp(jnp.sum(lq2_ref[...] * lk2_ref[...], axis=-1, keepdims=True)) + lam_init)
    denom = acc_ref[pl.ds(DIFF_VD, 1), :]
    o_both = acc_ref[pl.ds(0, DIFF_VD), :] * (1.0 / denom)
    ot = o_both[:, :tq] - lam * o_both[:, tq:]
    ot = ot * lax.rsqrt(jnp.mean(ot * ot, axis=0, keepdims=True) + LN_EPS)
    o_ref[...] = (ot.T * ng_ref[...] * (1.0 - lam_init)).astype(o_ref.dtype)


def _diff_attn(dqk, dv, lq1, lk1, lq2, lk2, norm_g, batch, seq, lam_init):
    n = dqk.shape[0]
    tq = min(ATT_TQ, seq)
    nq = seq // tq
    vec = pl.BlockSpec((1, DIFF_D), lambda b, h, i: (0, 0))
    return pl.pallas_call(
        functools.partial(_diff_attn_kernel, lam_init=lam_init),
        grid=(batch, DIFF_HEADS, nq),
        in_specs=[
            pl.BlockSpec((tq, DIFF_VD), lambda b, h, i: (b * nq + i, h)),
            pl.BlockSpec((seq, DIFF_VD), lambda b, h, i: (b, DIFF_HEADS + h)),
            pl.BlockSpec((seq, DIFF_VD), lambda b, h, i: (b, h)),
            vec, vec, vec, vec,
            pl.BlockSpec((1, DIFF_VD), lambda b, h, i: (0, h)),
        ],
        out_specs=pl.BlockSpec((tq, DIFF_VD), lambda b, h, i: (b * nq + i, h)),
        out_shape=jax.ShapeDtypeStruct((n, DIFF_HEADS * DIFF_VD), BF16),
        scratch_shapes=[pltpu.VMEM((nq, DIFF_VD + BF16_SUBLANES, tq), BF16),
                        pltpu.VMEM((tq, 2 * tq), F32), pltpu.VMEM((tq, 2 * tq), F32),
                        pltpu.VMEM((1, 2 * tq), F32), pltpu.VMEM((1, 2 * tq), F32),
                        pltpu.VMEM((1, 2 * tq), F32),
                        pltpu.VMEM((DIFF_VD + BF16_SUBLANES, 2 * tq), F32)],
        compiler_params=_params("parallel", "parallel", "arbitrary"),
        name="diff_attn",
    )(dqk, dqk, dv, lq1.reshape(1, -1), lk1.reshape(1, -1), lq2.reshape(1, -1), lk2.reshape(1, -1),
      norm_g.reshape(1, -1))


def _conv_kernel(a_ref, g_ref, ah_ref, gh_ref, w_ref, cb_ref, lg_ref, lb_ref, o_ref, h_ref, y_ref,
                 *, tiles_per_seq):
    tt = a_ref.shape[0]
    first = (pl.program_id(0) % tiles_per_seq) == 0
    halo = ah_ref[...] * _sigmoid(gh_ref[...])
    h_ref[pl.ds(0, CONV_HALO), :] = jnp.where(first, 0.0, halo)
    h_ref[pl.ds(CONV_HALO, tt), :] = a_ref[...] * _sigmoid(g_ref[...])
    base = CONV_HALO - (CONV_K - 1)
    rows = tt + CONV_HALO
    for cb in range(a_ref.shape[1] // LANES):
        cols = pl.ds(cb * LANES, LANES)
        hc = h_ref[:, cols]
        acc = jnp.zeros((tt, LANES), F32)
        for r in range(F32_SUBLANES):
            shifted = pltpu.roll(hc, rows - (base + r), axis=0)
            for k in range(r, CONV_K, F32_SUBLANES):
                acc = acc + w_ref[pl.ds(k, 1), cols] * shifted[k - r:k - r + tt, :]
        y_ref[:, cols] = acc + cb_ref[:, cols]
    y = _layer_norm_rows(y_ref[...], lg_ref[...], lb_ref[...])
    o_ref[...] = (y * _sigmoid(y)).astype(o_ref.dtype)


def _conv(cacg, w_dw, b_dw, ln_g, ln_b, seq):
    n = cacg.shape[0]
    c = BR_WIDTH
    tt = min(CONV_TT, seq)
    ratio = tt // CONV_HALO
    row = lambda v: v.reshape(1, c)
    return pl.pallas_call(
        functools.partial(_conv_kernel, tiles_per_seq=seq // tt),
        grid=(n // tt,),
        in_specs=[
            pl.BlockSpec((tt, c), lambda i: (i, 0)),
            pl.BlockSpec((tt, c), lambda i: (i, 1)),
            pl.BlockSpec((CONV_HALO, c), lambda i: (jnp.maximum(i * ratio - 1, 0), 0)),
            pl.BlockSpec((CONV_HALO, c), lambda i: (jnp.maximum(i * ratio - 1, 0), 1)),
            pl.BlockSpec((CONV_K, c), lambda i: (0, 0)),
            pl.BlockSpec((1, c), lambda i: (0, 0)),
            pl.BlockSpec((1, c), lambda i: (0, 0)),
            pl.BlockSpec((1, c), lambda i: (0, 0)),
        ],
        out_specs=pl.BlockSpec((tt, c), lambda i: (i, 0)),
        out_shape=jax.ShapeDtypeStruct((n, c), BF16),
        scratch_shapes=[pltpu.VMEM((tt + CONV_HALO, c), F32), pltpu.VMEM((tt, c), F32)],
        compiler_params=_params("parallel"),
        name="conformer_conv",
    )(cacg, cacg, cacg, cacg, w_dw, row(b_dw), row(ln_g), row(ln_b))


def _sb_attn_kernel(q_ref, k_ref, v_ref, o_ref, vt_ref, logsig0_ref, logsig1_ref, terms0_ref, terms1_ref,
                    tot0_ref, tot1_ref, rest_ref, acc_ref, *, tk):
    qi = pl.program_id(2)
    tq = q_ref.shape[0]
    q = q_ref[...]

    @pl.when(qi == 0)
    def _():
        _store_transposed_blocks(v_ref, vt_ref)

    to_log2 = SB_D ** -0.5 * math.log2(math.e)
    key_a = lax.broadcasted_iota(jnp.int32, (tk, tk), 0)
    key_b = lax.broadcasted_iota(jnp.int32, (tk, tk), 1)
    later = jnp.where(key_b > key_a, 1.0, 0.0).astype(BF16)
    suffix_ones = jnp.concatenate([later, later], axis=1)

    rest_ref[...] = jnp.zeros(rest_ref.shape, F32)
    acc_ref[...] = jnp.zeros(acc_ref.shape, F32)
    slots = ((logsig0_ref, terms0_ref, tot0_ref), (logsig1_ref, terms1_ref, tot1_ref))
    nsub = tq // tk

    def prepare(j, slot, diagonal):
        logsig_ref, terms_ref, tot_ref = slots[slot]
        kb = k_ref[pl.ds(pl.multiple_of(j * tq, tq), tq), :]
        z = lax.dot_general(kb, q, (((1,), (1,)), ((), ())), preferred_element_type=F32) * to_log2
        log_rest = -(jnp.maximum(z, 0.0) + jnp.log2(1.0 + jnp.exp2(-jnp.abs(z))))
        logsig = z + log_rest
        if diagonal:
            mask = (lax.broadcasted_iota(jnp.int32, (tq, tq), 0)
                    < lax.broadcasted_iota(jnp.int32, (tq, tq), 1))
            log_rest = jnp.where(mask, log_rest, 0.0)
            logsig = jnp.where(mask, logsig, -1e30)
        logsig_ref[...] = logsig
        hi = log_rest.astype(BF16)
        lo = (log_rest - hi.astype(F32)).astype(BF16)
        for sub in range(nsub):
            rows = slice(sub * tk, (sub + 1) * tk)
            terms_ref[sub] = jnp.concatenate([hi[rows], lo[rows]], axis=0)
            tot_ref[pl.ds(sub, 1), :] = jnp.sum(log_rest[rows], axis=0, keepdims=True)

    def accumulate(j, slot):
        logsig_ref, terms_ref, tot_ref = slots[slot]
        offset = rest_ref[...]
        pieces = []
        for sub in reversed(range(nsub)):
            suffix = jnp.dot(suffix_ones, terms_ref[sub], preferred_element_type=F32)
            logw = logsig_ref[pl.ds(sub * tk, tk), :] + suffix + offset
            pieces.insert(0, jnp.exp2(logw).astype(BF16))
            offset = offset + tot_ref[pl.ds(sub, 1), :]
        acc_ref[...] += jnp.dot(vt_ref[j], jnp.concatenate(pieces, axis=0), preferred_element_type=F32)
        rest_ref[...] = offset

    prepare(qi, 0, True)

    def body(jj, carry):
        for slot in range(2):
            @pl.when((jj & 1) == slot)
            def _(slot=slot):
                prepare(qi - 1 - jj, 1 - slot, False)
                accumulate(qi - jj, slot)
        return carry

    lax.fori_loop(0, qi, body, 0)
    for slot in range(2):
        @pl.when((qi & 1) == slot)
        def _(slot=slot):
            accumulate(0, slot)
    o_ref[...] = acc_ref[...].T.astype(o_ref.dtype)


def _sb_attn(sqkv, batch, seq):
    n = sqkv.shape[0]
    tq = min(ATT_TQ, seq)
    nq = seq // tq
    tk = min(SB_TK, tq)
    return pl.pallas_call(
        functools.partial(_sb_attn_kernel, tk=tk),
        grid=(batch, SB_HEADS, nq),
        in_specs=[
            pl.BlockSpec((tq, SB_D), lambda b, h, i: (b * nq + i, h)),
            pl.BlockSpec((seq, SB_D), lambda b, h, i: (b, SB_HEADS + h)),
            pl.BlockSpec((seq, SB_D), lambda b, h, i: (b, 2 * SB_HEADS + h)),
        ],
        out_specs=pl.BlockSpec((tq, SB_D), lambda b, h, i: (b * nq + i, h)),
        out_shape=jax.ShapeDtypeStruct((n, SB_HEADS * SB_D), BF16),
        scratch_shapes=[pltpu.VMEM((nq, SB_D, tq), BF16),
                        pltpu.VMEM((tq, tq), F32), pltpu.VMEM((tq, tq), F32),
                        pltpu.VMEM((tq // tk, 2 * tk, tq), BF16), pltpu.VMEM((tq // tk, 2 * tk, tq), BF16),
                        pltpu.VMEM((F32_SUBLANES, tq), F32), pltpu.VMEM((F32_SUBLANES, tq), F32),
                        pltpu.VMEM((1, tq), F32), pltpu.VMEM((SB_D, tq), F32)],
        compiler_params=_params("parallel", "parallel", "arbitrary"),
        name="sb_attn",
    )(sqkv, sqkv, sqkv)


def _merge_kernel(h_ref, oa_ref, ob_ref, oc_ref, od_ref, wga_ref, wgb_ref, wgc_ref, wgd_ref, wb_ref, o_ref):
    h = h_ref[...]
    merged = None
    for idx, (br_ref, wg_ref) in enumerate(((oa_ref, wga_ref), (ob_ref, wgb_ref),
                                            (oc_ref, wgc_ref), (od_ref, wgd_ref))):
        gate = _sigmoid(jnp.dot(h, wg_ref[...], preferred_element_type=F32))
        term = gate * jnp.dot(br_ref[...], wb_ref[idx], preferred_element_type=F32)
        merged = term if merged is None else merged + term
    o_ref[...] = merged.astype(o_ref.dtype)


def _merge(hb, branches, w_in, w_branch, layer):
    n, d = hb.shape
    tm = min(MERGE_TM, n)
    tn = MERGE_TN
    nc = d // tn
    gate_off = COL_GATE // tn
    br_spec = pl.BlockSpec((tm, BR_WIDTH), lambda i, c: (i, 0))

    def gate_spec(b):
        return pl.BlockSpec((None, d, tn), lambda i, c: (layer, 0, gate_off + b * nc + c))

    return pl.pallas_call(
        _merge_kernel,
        grid=(n // tm, nc),
        in_specs=[
            pl.BlockSpec((tm, d), lambda i, c: (i, 0)),
            br_spec, br_spec, br_spec, br_spec,
            gate_spec(0), gate_spec(1), gate_spec(2), gate_spec(3),
            pl.BlockSpec((None, N_BRANCH, BR_WIDTH, tn), lambda i, c: (layer, 0, 0, c)),
        ],
        out_specs=pl.BlockSpec((tm, tn), lambda i, c: (i, c)),
        out_shape=jax.ShapeDtypeStruct((n, d), BF16),
        compiler_params=_params("parallel", "arbitrary"),
        name="merge",
    )(hb, *branches, w_in, w_in, w_in, w_in, w_branch)


def _out_ln_kernel(m_ref, w_ref, x_ref, g_ref, b_ref, o_ref):
    y = ALPHA * x_ref[...] + jnp.dot(m_ref[...], w_ref[...], preferred_element_type=F32)
    o_ref[...] = _layer_norm_rows(y, g_ref[...], b_ref[...])


def _out_ln(merged, w_out, layer, x, ln_g, ln_b):
    n, d = x.shape
    tm = min(OUT_TM, n)
    return pl.pallas_call(
        _out_ln_kernel,
        grid=(n // tm,),
        in_specs=[
            pl.BlockSpec((tm, d), lambda i: (i, 0)),
            pl.BlockSpec((None, d, d), lambda i: (layer, 0, 0)),
            pl.BlockSpec((tm, d), lambda i: (i, 0)),
            pl.BlockSpec((1, d), lambda i: (0, 0)),
            pl.BlockSpec((1, d), lambda i: (0, 0)),
        ],
        out_specs=pl.BlockSpec((tm, d), lambda i: (i, 0)),
        out_shape=jax.ShapeDtypeStruct((n, d), F32),
        compiler_params=_params("parallel"),
        name="out_proj_ln",
    )(merged, w_out, x, ln_g.reshape(1, d), ln_b.reshape(1, d))


def kernel(x, positions, ffn1_w_in, ffn1_w_out, ln1_g, ln1_b, w_in, gmlp_ln_g, gmlp_ln_b, gmlp_ws, gmlp_bs,
           diff_lq1, diff_lk1, diff_lq2, diff_lk2, diff_norm_g, conv_w, conv_b, conv_ln_g, conv_ln_b,
           w_branch, w_out, ln2_g, ln2_b, ffn2_w_in, ffn2_w_out, ln3_g, ln3_b):
    batch, seq, d = x.shape
    n = batch * seq
    depth = w_in.shape[0]
    ffn1_w_in, ffn1_w_out, ffn2_w_in, ffn2_w_out, w_in, w_branch, w_out = (
        w.astype(BF16) for w in (ffn1_w_in, ffn1_w_out, ffn2_w_in, ffn2_w_out, w_in, w_branch, w_out))

    rope = _rope_tables(positions)
    xf = x.reshape(n, d)
    for i in range(depth):
        lam_init = 0.8 - 0.6 * math.exp(-0.3 * i)
        xf, xb = _ffn(xf, ffn1_w_in, ffn1_w_out, i, ln1_g[i], ln1_b[i])

        uv = _proj(xb, w_in, i, COL_UV, 2 * BR_WIDTH, F32)
        dqk = _proj(xb, w_in, i, COL_DQK, 2 * BR_WIDTH, BF16, rope=rope)
        dv = _proj(xb, w_in, i, COL_DV, BR_WIDTH, BF16)
        cacg = _proj(xb, w_in, i, COL_CONV, 2 * BR_WIDTH, F32)
        sqkv = _proj(xb, w_in, i, COL_SB, 3 * BR_WIDTH, BF16)

        o_a = _gmlp(uv, gmlp_ln_g[i], gmlp_ln_b[i], gmlp_ws[i], gmlp_bs[i])
        o_b = _diff_attn(dqk, dv, diff_lq1[i], diff_lk1[i], diff_lq2[i], diff_lk2[i], diff_norm_g[i],
                         batch, seq, lam_init)
        o_c = _conv(cacg, conv_w[i], conv_b[i], conv_ln_g[i], conv_ln_b[i], seq)
        o_d = _sb_attn(sqkv, batch, seq)

        merged = _merge(xb, (o_a, o_b, o_c, o_d), w_in, w_branch, i)
        xf = _out_ln(merged, w_out, i, xf, ln2_g[i], ln2_b[i])
        xf, _ = _ffn(xf, ffn2_w_in, ffn2_w_out, i, ln3_g[i], ln3_b[i])
    return xf.reshape(batch, seq, d)
```

```python
import functools
import math

import jax
import jax.numpy as jnp
from jax import lax
from jax.experimental import pallas as pl
from jax.experimental.pallas import tpu as pltpu

F32 = jnp.float32
BF16 = jnp.bfloat16

D_MODEL = 2048
FF_DIM = 5632
CHUNK = 64
N_BRANCH = 4
BR_WIDTH = 1024
GMLP_BLOCK = 128
GMLP_GROUPS = 4
GMLP_GC = BR_WIDTH // GMLP_GROUPS
DIFF_HEADS = 8
DIFF_D = 64
DIFF_VD = 2 * DIFF_D
SB_HEADS = 8
SB_D = 128
CONV_K = 31
ROT_DIM = DIFF_D // 4
ROPE_THETA = 500000.0
MODEL_DEPTH = 4
ALPHA = (2 * MODEL_DEPTH) ** 0.25
LN_EPS = 1e-5
COL_U = 0
COL_V = BR_WIDTH
COL_DQ = 2 * BR_WIDTH
COL_DK = 3 * BR_WIDTH
COL_DV = 4 * BR_WIDTH
COL_CA = 5 * BR_WIDTH
COL_CG = 6 * BR_WIDTH
COL_SQ = 7 * BR_WIDTH
COL_SK = 8 * BR_WIDTH
COL_SV = 9 * BR_WIDTH
COL_GATE = 10 * BR_WIDTH

LANES = 128
F32_SUBLANES = 8
BF16_SUBLANES = 16
VMEM_LIMIT_BYTES = 56 * 1024 * 1024

FFN_TM = 512
FFN_TF = 512
PROJ_TM = 1024
PROJ_TN = 1024
MERGE_TM = 512
MERGE_TN = 512
OUT_TM = 512
GMLP_TM = 512
CONV_TT = 256
CONV_HALO = 32
ATT_TQ = 512
SB_TK = 256
ROPE_TM = 1024


def _params(*sem):
    return pltpu.CompilerParams(dimension_semantics=sem, vmem_limit_bytes=VMEM_LIMIT_BYTES)


def _sigmoid(x):
    return 1.0 / (1.0 + jnp.exp(-x))


def _chunk_of(size, axis):
    return lax.broadcasted_iota(jnp.int32, (size, size), axis) >> int(math.log2(CHUNK))


def _layer_norm_rows(y, g, b):
    mu = jnp.mean(y, axis=-1, keepdims=True)
    yc = y - mu
    var = jnp.mean(yc * yc, axis=-1, keepdims=True)
    return yc * lax.rsqrt(var + LN_EPS) * g + b


def _ffn_kernel(x_ref, wa_ref, wg_ref, wo_ref, g_ref, b_ref, of_ref, ob_ref, xb_ref, acc_ref,
                act0_ref, act1_ref, *, nf):
    j = pl.program_id(1)
    acts = (act0_ref, act1_ref)

    def activate(dst_ref):
        xb = xb_ref[...]
        a = jnp.dot(xb, wa_ref[...], preferred_element_type=F32)
        gt = jnp.dot(xb, wg_ref[...], preferred_element_type=F32)
        dst_ref[...] = (a * _sigmoid(a) * gt).astype(BF16)

    def project_down(src_ref):
        acc_ref[...] += jnp.dot(src_ref[...], wo_ref[...], preferred_element_type=F32)

    @pl.when(j == 0)
    def _():
        xb_ref[...] = x_ref[...].astype(BF16)
        acc_ref[...] = jnp.zeros(acc_ref.shape, F32)
        activate(acts[0])

    for slot in range(2):
        @pl.when((j > 0) & (j < nf) & ((j & 1) == slot))
        def _(slot=slot):
            project_down(acts[1 - slot])
            activate(acts[slot])

    @pl.when(j == nf)
    def _():
        project_down(acts[(nf - 1) % 2])
        y = ALPHA * x_ref[...] + 0.5 * acc_ref[...]
        out = _layer_norm_rows(y, g_ref[...], b_ref[...])
        of_ref[...] = out
        ob_ref[...] = out.astype(BF16)


def _ffn(x, w_in, w_out, layer, ln_g, ln_b):
    n, d = x.shape
    f = w_out.shape[1]
    tm = min(FFN_TM, n)
    tf = FFN_TF
    nf = f // tf
    last = nf - 1
    return pl.pallas_call(
        functools.partial(_ffn_kernel, nf=nf),
        grid=(n // tm, nf + 1),
        in_specs=[
            pl.BlockSpec((tm, d), lambda i, j: (i, 0)),
            pl.BlockSpec((None, d, tf), lambda i, j: (layer, 0, jnp.minimum(j, last))),
            pl.BlockSpec((None, d, tf), lambda i, j: (layer, 0, jnp.minimum(j, last) + nf)),
            pl.BlockSpec((None, tf, d), lambda i, j: (layer, jnp.maximum(j - 1, 0), 0)),
            pl.BlockSpec((1, d), lambda i, j: (0, 0)),
            pl.BlockSpec((1, d), lambda i, j: (0, 0)),
        ],
        out_specs=[
            pl.BlockSpec((tm, d), lambda i, j: (i, 0)),
            pl.BlockSpec((tm, d), lambda i, j: (i, 0)),
        ],
        out_shape=[jax.ShapeDtypeStruct((n, d), F32), jax.ShapeDtypeStruct((n, d), BF16)],
        scratch_shapes=[pltpu.VMEM((tm, d), BF16), pltpu.VMEM((tm, d), F32),
                        pltpu.VMEM((tm, tf), BF16), pltpu.VMEM((tm, tf), BF16)],
        compiler_params=_params("parallel", "arbitrary"),
        name="ffn",
    )(x, w_in, w_in, w_out, ln_g.reshape(1, d), ln_b.reshape(1, d))


def _rope_table_kernel(pos_ref, invf_ref, c_ref, s1_ref, s2_ref):
    ang = pos_ref[...] * invf_ref[...]
    cos = jnp.cos(ang)
    sin = jnp.sin(ang)
    lane = lax.broadcasted_iota(jnp.int32, ang.shape, 1) & (DIFF_D - 1)
    half = ROT_DIM // 2
    c_ref[...] = jnp.where(lane < ROT_DIM, cos, 1.0)
    s1_ref[...] = jnp.where(lane < half, -sin, 0.0)
    s2_ref[...] = jnp.where((lane >= half) & (lane < ROT_DIM), sin, 0.0)


def _rope_tables(positions):
    n = positions.size
    tm = min(ROPE_TM, n)
    half = ROT_DIM // 2
    inv_freq = ROPE_THETA ** (-jnp.arange(0, ROT_DIM, 2, dtype=F32) / ROT_DIM)
    per_head = jnp.concatenate([inv_freq, inv_freq, jnp.zeros((DIFF_D - ROT_DIM,), F32)])
    invf = jnp.tile(per_head, LANES // DIFF_D).reshape(1, LANES)
    pos = positions.reshape(n, 1).astype(F32)
    tab = jax.ShapeDtypeStruct((n, LANES), F32)
    return pl.pallas_call(
        _rope_table_kernel,
        grid=(n // tm,),
        in_specs=[pl.BlockSpec((tm, 1), lambda i: (i, 0)), pl.BlockSpec((1, LANES), lambda i: (0, 0))],
        out_specs=[pl.BlockSpec((tm, LANES), lambda i: (i, 0))] * 3,
        out_shape=[tab, tab, tab],
        compiler_params=_params("parallel"),
        name="rope_tables",
    )(pos, invf)


def _proj_kernel(x_ref, w_ref, c_ref, s1_ref, s2_ref, o_ref, *, q_tile, k_tile, q_scale):
    j = pl.program_id(1)
    acc = jnp.dot(x_ref[...], w_ref[...], preferred_element_type=F32)
    rotary = (j == q_tile) | (j == k_tile)

    @pl.when(jnp.logical_not(rotary))
    def _():
        o_ref[...] = acc.astype(o_ref.dtype)

    @pl.when(rotary)
    def _():
        tn = acc.shape[1]
        reps = tn // LANES
        c = jnp.concatenate([c_ref[...]] * reps, axis=1)
        s1 = jnp.concatenate([s1_ref[...]] * reps, axis=1)
        s2 = jnp.concatenate([s2_ref[...]] * reps, axis=1)
        half = ROT_DIM // 2
        nxt = pltpu.roll(acc, tn - half, axis=1)
        prv = pltpu.roll(acc, half, axis=1)
        out = acc * c + nxt * s1 + prv * s2
        scale = jnp.where(j == q_tile, q_scale, 1.0).astype(F32)
        o_ref[...] = (out * scale).astype(o_ref.dtype)


def _proj(xb, w, layer, rope):
    n, d = xb.shape
    tm = min(PROJ_TM, n)
    tn = PROJ_TN
    t_spec = pl.BlockSpec((tm, LANES), lambda i, j: (i, 0))
    kern = functools.partial(_proj_kernel, q_tile=COL_DQ // tn, k_tile=COL_DK // tn, q_scale=DIFF_D ** -0.5)
    return pl.pallas_call(
        kern,
        grid=(n // tm, COL_GATE // tn),
        in_specs=[pl.BlockSpec((tm, d), lambda i, j: (i, 0)),
                  pl.BlockSpec((None, d, tn), lambda i, j: (layer, 0, j)),
                  t_spec, t_spec, t_spec],
        out_specs=pl.BlockSpec((tm, tn), lambda i, j: (i, j)),
        out_shape=jax.ShapeDtypeStruct((n, COL_GATE), BF16),
        compiler_params=_params("parallel", "arbitrary"),
        name="proj",
    )(xb, w, *rope)


def _gmlp_kernel(u_ref, v_ref, g_ref, b_ref, ws_ref, bs_ref, o_ref):
    tm = u_ref.shape[0]
    visible = _chunk_of(GMLP_BLOCK, 1) <= _chunk_of(GMLP_BLOCK, 0)
    g = g_ref[...]
    b = b_ref[...]
    for blk in range(tm // GMLP_BLOCK):
        rows = pl.ds(blk * GMLP_BLOCK, GMLP_BLOCK)
        vn = _layer_norm_rows(v_ref[rows, :].astype(F32), g, b).astype(BF16)
        for grp in range(GMLP_GROUPS):
            cols = pl.ds(grp * GMLP_GC, GMLP_GC)
            w = jnp.where(visible, ws_ref[grp], 0.0).astype(BF16)
            mixed = jnp.dot(w, vn[:, grp * GMLP_GC:(grp + 1) * GMLP_GC], preferred_element_type=F32)
            mixed = mixed + bs_ref[:, grp:grp + 1]
            o_ref[rows, cols] = (u_ref[rows, cols].astype(F32) * mixed).astype(o_ref.dtype)


def _gmlp(proj, ln_g, ln_b, ws, bs):
    n = proj.shape[0]
    tm = min(GMLP_TM, n)
    return pl.pallas_call(
        _gmlp_kernel,
        grid=(n // tm,),
        in_specs=[
            pl.BlockSpec((tm, BR_WIDTH), lambda i: (i, COL_U // BR_WIDTH)),
            pl.BlockSpec((tm, BR_WIDTH), lambda i: (i, COL_V // BR_WIDTH)),
            pl.BlockSpec((1, BR_WIDTH), lambda i: (0, 0)),
            pl.BlockSpec((1, BR_WIDTH), lambda i: (0, 0)),
            pl.BlockSpec((GMLP_GROUPS, GMLP_BLOCK, GMLP_BLOCK), lambda i: (0, 0, 0)),
            pl.BlockSpec((GMLP_BLOCK, GMLP_GROUPS), lambda i: (0, 0)),
        ],
        out_specs=pl.BlockSpec((tm, BR_WIDTH), lambda i: (i, 0)),
        out_shape=jax.ShapeDtypeStruct((n, BR_WIDTH), BF16),
        compiler_params=_params("parallel"),
        name="gmlp",
    )(proj, proj, ln_g.reshape(1, -1), ln_b.reshape(1, -1), ws, bs.T)


def _store_transposed_blocks(v_ref, vt_ref):
    nblk, rows, tk = vt_ref.shape
    d = v_ref.shape[1]
    for c in range(nblk):
        blk = v_ref[pl.ds(c * tk, tk), :].astype(F32)
        vt_ref[c, pl.ds(0, d), :] = blk.T.astype(vt_ref.dtype)
        if rows > d:
            first = lax.broadcasted_iota(jnp.int32, (rows - d, tk), 0) == 0
            vt_ref[c, pl.ds(d, rows - d), :] = jnp.where(first, 1.0, 0.0).astype(vt_ref.dtype)


def _diff_attn_kernel(q_ref, k_ref, v_ref, lq1_ref, lk1_ref, lq2_ref, lk2_ref, ng_ref, o_ref,
                      vt_ref, s0_ref, s1_ref, bmax0_ref, bmax1_ref, m_ref, acc_ref, *, lam_init):
    qi = pl.program_id(2)
    tq = q_ref.shape[0]
    q = q_ref[...]
    lane = lax.broadcasted_iota(jnp.int32, q.shape, 1)
    q_both = jnp.concatenate([jnp.where(lane < DIFF_D, q, 0), jnp.where(lane >= DIFF_D, q, 0)], axis=0)

    @pl.when(qi == 0)
    def _():
        _store_transposed_blocks(v_ref, vt_ref)

    m_ref[...] = jnp.full(m_ref.shape, -1e30, F32)
    acc_ref[...] = jnp.zeros(acc_ref.shape, F32)

    s_slots = ((s0_ref, bmax0_ref), (s1_ref, bmax1_ref))

    def scores(j, slot):
        s_ref, bmax_ref = s_slots[slot]
        kb = k_ref[pl.ds(pl.multiple_of(j * tq, tq), tq), :]
        s = lax.dot_general(kb, q_both, (((1,), (1,)), ((), ())), preferred_element_type=F32)
        s_ref[...] = s
        bmax_ref[...] = jnp.max(s, axis=0, keepdims=True)

    def softmax_pv(j, slot, visible):
        s_ref, bmax_ref = s_slots[slot]
        s = s_ref[...]
        if visible is None:
            block_max = bmax_ref[...]
        else:
            s = jnp.where(visible, s, -jnp.inf)
            block_max = jnp.max(s, axis=0, keepdims=True)
        m_old = m_ref[...]
        m_new = jnp.maximum(m_old, block_max)
        p = jnp.exp(s - m_new)
        corr = jnp.exp(m_old - m_new)
        acc_ref[...] = corr * acc_ref[...] + jnp.dot(vt_ref[j], p.astype(BF16), preferred_element_type=F32)
        m_ref[...] = m_new

    scores(0, 0)

    def body(j, carry):
        for slot in range(2):
            @pl.when((j & 1) == slot)
            def _(slot=slot):
                scores(j + 1, 1 - slot)
                softmax_pv(j, slot, None)
        return carry

    lax.fori_loop(0, qi, body, 0)
    visible = _chunk_of(tq, 0) <= _chunk_of(tq, 1)
    visible = jnp.concatenate([visible, visible], axis=1)
    for slot in range(2):
        @pl.when((qi & 1) == slot)
        def _(slot=slot):
            softmax_pv(qi, slot, visible)

    lam = (jnp.exp(jnp.sum(lq1_ref[...] * lk1_ref[...], axis=-1, keepdims=True))
           - jnp.exp(jnp.sum(lq2_ref[...] * lk2_ref[...], axis=-1, keepdims=True)) + lam_init)
    denom = acc_ref[pl.ds(DIFF_VD, 1), :]
    o_both = acc_ref[pl.ds(0, DIFF_VD), :] * (1.0 / denom)
    ot = o_both[:, :tq] - lam * o_both[:, tq:]
    ot = ot * lax.rsqrt(jnp.mean(ot * ot, axis=0, keepdims=True) + LN_EPS)
    o_ref[...] = (ot.T * ng_ref[...] * (1.0 - lam_init)).astype(o_ref.dtype)


def _diff_attn(proj, lq1, lk1, lq2, lk2, norm_g, batch, seq, lam_init):
    n = proj.shape[0]
    tq = min(ATT_TQ, seq)
    nq = seq // tq
    vec = pl.BlockSpec((1, DIFF_D), lambda b, h, i: (0, 0))
    q0, k0, v0 = COL_DQ // DIFF_VD, COL_DK // DIFF_VD, COL_DV // DIFF_VD
    return pl.pallas_call(
        functools.partial(_diff_attn_kernel, lam_init=lam_init),
        grid=(batch, DIFF_HEADS, nq),
        in_specs=[
            pl.BlockSpec((tq, DIFF_VD), lambda b, h, i: (b * nq + i, q0 + h)),
            pl.BlockSpec((seq, DIFF_VD), lambda b, h, i: (b, k0 + h)),
            pl.BlockSpec((seq, DIFF_VD), lambda b, h, i: (b, v0 + h)),
            vec, vec, vec, vec,
            pl.BlockSpec((1, DIFF_VD), lambda b, h, i: (0, h)),
        ],
        out_specs=pl.BlockSpec((tq, DIFF_VD), lambda b, h, i: (b * nq + i, h)),
        out_shape=jax.ShapeDtypeStruct((n, DIFF_HEADS * DIFF_VD), BF16),
        scratch_shapes=[pltpu.VMEM((nq, DIFF_VD + BF16_SUBLANES, tq), BF16),
                        pltpu.VMEM((tq, 2 * tq), F32), pltpu.VMEM((tq, 2 * tq), F32),
                        pltpu.VMEM((1, 2 * tq), F32), pltpu.VMEM((1, 2 * tq), F32),
                        pltpu.VMEM((1, 2 * tq), F32),
                        pltpu.VMEM((DIFF_VD + BF16_SUBLANES, 2 * tq), F32)],
        compiler_params=_params("parallel", "parallel", "arbitrary"),
        name="diff_attn",
    )(proj, proj, proj, lq1.reshape(1, -1), lk1.reshape(1, -1), lq2.reshape(1, -1), lk2.reshape(1, -1),
      norm_g.reshape(1, -1))


def _conv_kernel(a_ref, g_ref, ah_ref, gh_ref, w_ref, cb_ref, lg_ref, lb_ref, o_ref, h_ref, y_ref,
                 *, tiles_per_seq):
    tt = a_ref.shape[0]
    first = (pl.program_id(0) % tiles_per_seq) == 0
    halo = ah_ref[...].astype(F32) * _sigmoid(gh_ref[...].astype(F32))
    h_ref[pl.ds(0, CONV_HALO), :] = jnp.where(first, 0.0, halo)
    h_ref[pl.ds(CONV_HALO, tt), :] = a_ref[...].astype(F32) * _sigmoid(g_ref[...].astype(F32))
    base = CONV_HALO - (CONV_K - 1)
    rows = tt + CONV_HALO
    for cb in range(a_ref.shape[1] // LANES):
        cols = pl.ds(cb * LANES, LANES)
        hc = h_ref[:, cols]
        acc = jnp.zeros((tt, LANES), F32)
        for r in range(F32_SUBLANES):
            shifted = pltpu.roll(hc, rows - (base + r), axis=0)
            for k in range(r, CONV_K, F32_SUBLANES):
                acc = acc + w_ref[pl.ds(k, 1), cols] * shifted[k - r:k - r + tt, :]
        y_ref[:, cols] = acc + cb_ref[:, cols]
    y = _layer_norm_rows(y_ref[...], lg_ref[...], lb_ref[...])
    o_ref[...] = (y * _sigmoid(y)).astype(o_ref.dtype)


def _conv(proj, w_dw, b_dw, ln_g, ln_b, seq):
    n = proj.shape[0]
    c = BR_WIDTH
    tt = min(CONV_TT, seq)
    ratio = tt // CONV_HALO
    row = lambda v: v.reshape(1, c)
    ca, cg = COL_CA // c, COL_CG // c
    return pl.pallas_call(
        functools.partial(_conv_kernel, tiles_per_seq=seq // tt),
        grid=(n // tt,),
        in_specs=[
            pl.BlockSpec((tt, c), lambda i: (i, ca)),
            pl.BlockSpec((tt, c), lambda i: (i, cg)),
            pl.BlockSpec((CONV_HALO, c), lambda i: (jnp.maximum(i * ratio - 1, 0), ca)),
            pl.BlockSpec((CONV_HALO, c), lambda i: (jnp.maximum(i * ratio - 1, 0), cg)),
            pl.BlockSpec((CONV_K, c), lambda i: (0, 0)),
            pl.BlockSpec((1, c), lambda i: (0, 0)),
            pl.BlockSpec((1, c), lambda i: (0, 0)),
            pl.BlockSpec((1, c), lambda i: (0, 0)),
        ],
        out_specs=pl.BlockSpec((tt, c), lambda i: (i, 0)),
        out_shape=jax.ShapeDtypeStruct((n, c), BF16),
        scratch_shapes=[pltpu.VMEM((tt + CONV_HALO, c), F32), pltpu.VMEM((tt, c), F32)],
        compiler_params=_params("parallel"),
        name="conformer_conv",
    )(proj, proj, proj, proj, w_dw, row(b_dw), row(ln_g), row(ln_b))


def _sb_attn_kernel(q_ref, k_ref, v_ref, o_ref, vt_ref, logw0_ref, logw1_ref, tot0_ref, tot1_ref,
                    rest_ref, acc_ref, *, tk):
    qi = pl.program_id(2)
    tq = q_ref.shape[0]
    q = q_ref[...]

    @pl.when(qi == 0)
    def _():
        _store_transposed_blocks(v_ref, vt_ref)

    to_log2 = SB_D ** -0.5 * math.log2(math.e)
    key_a = lax.broadcasted_iota(jnp.int32, (tk, tk), 0)
    key_b = lax.broadcasted_iota(jnp.int32, (tk, tk), 1)
    later = jnp.where(key_b > key_a, 1.0, 0.0).astype(BF16)

    rest_ref[...] = jnp.zeros(rest_ref.shape, F32)
    acc_ref[...] = jnp.zeros(acc_ref.shape, F32)
    slots = ((logw0_ref, tot0_ref), (logw1_ref, tot1_ref))

    def prepare(j, slot, diagonal):
        logw_ref, tot_ref = slots[slot]
        kb = k_ref[pl.ds(pl.multiple_of(j * tq, tq), tq), :]
        z = lax.dot_general(kb, q, (((1,), (1,)), ((), ())), preferred_element_type=F32) * to_log2
        log_rest = -(jnp.maximum(z, 0.0) + jnp.log2(1.0 + jnp.exp2(-jnp.abs(z))))
        if diagonal:
            mask = (lax.broadcasted_iota(jnp.int32, (tq, tq), 0)
                    < lax.broadcasted_iota(jnp.int32, (tq, tq), 1))
            log_rest = jnp.where(mask, log_rest, 0.0)
        terms = log_rest.astype(BF16)
        running = jnp.zeros((1, tq), F32)
        for sub in reversed(range(tq // tk)):
            rows = slice(sub * tk, (sub + 1) * tk)
            suffix = jnp.dot(later, terms[rows], preferred_element_type=F32)
            logw = (z[rows] + log_rest[rows]) + suffix + running
            if diagonal:
                logw = jnp.where(mask[rows], logw, -1e30)
            logw_ref[rows, :] = logw
            running = running + jnp.sum(log_rest[rows], axis=0, keepdims=True)
        tot_ref[...] = running

    def accumulate(j, slot):
        logw_ref, tot_ref = slots[slot]
        rest = rest_ref[...]
        w = jnp.exp2(logw_ref[...] + rest)
        acc_ref[...] += jnp.dot(vt_ref[j], w.astype(BF16), preferred_element_type=F32)
        rest_ref[...] = rest + tot_ref[...]

    prepare(qi, 0, True)

    def body(jj, carry):
        for slot in range(2):
            @pl.when((jj & 1) == slot)
            def _(slot=slot):
                prepare(qi - 1 - jj, 1 - slot, False)
                accumulate(qi - jj, slot)
        return carry

    lax.fori_loop(0, qi, body, 0)
    for slot in range(2):
        @pl.when((qi & 1) == slot)
        def _(slot=slot):
            accumulate(0, slot)
    o_ref[...] = acc_ref[...].T.astype(o_ref.dtype)


def _sb_attn(proj, batch, seq):
    n = proj.shape[0]
    tq = min(ATT_TQ, seq)
    nq = seq // tq
    tk = min(SB_TK, tq)
    q0, k0, v0 = COL_SQ // SB_D, COL_SK // SB_D, COL_SV // SB_D
    return pl.pallas_call(
        functools.partial(_sb_attn_kernel, tk=tk),
        grid=(batch, SB_HEADS, nq),
        in_specs=[
            pl.BlockSpec((tq, SB_D), lambda b, h, i: (b * nq + i, q0 + h)),
            pl.BlockSpec((seq, SB_D), lambda b, h, i: (b, k0 + h)),
            pl.BlockSpec((seq, SB_D), lambda b, h, i: (b, v0 + h)),
        ],
        out_specs=pl.BlockSpec((tq, SB_D), lambda b, h, i: (b * nq + i, h)),
        out_shape=jax.ShapeDtypeStruct((n, SB_HEADS * SB_D), BF16),
        scratch_shapes=[pltpu.VMEM((nq, SB_D, tq), BF16),
                        pltpu.VMEM((tq, tq), F32), pltpu.VMEM((tq, tq), F32),
                        pltpu.VMEM((1, tq), F32), pltpu.VMEM((1, tq), F32),
                        pltpu.VMEM((1, tq), F32), pltpu.VMEM((SB_D, tq), F32)],
        compiler_params=_params("parallel", "parallel", "arbitrary"),
        name="sb_attn",
    )(proj, proj, proj)


def _merge_kernel(h_ref, oa_ref, ob_ref, oc_ref, od_ref, wga_ref, wgb_ref, wgc_ref, wgd_ref, wb_ref, o_ref):
    h = h_ref[...]
    merged = None
    for idx, (br_ref, wg_ref) in enumerate(((oa_ref, wga_ref), (ob_ref, wgb_ref),
                                            (oc_ref, wgc_ref), (od_ref, wgd_ref))):
        gate = _sigmoid(jnp.dot(h, wg_ref[...], preferred_element_type=F32))
        term = gate * jnp.dot(br_ref[...], wb_ref[idx], preferred_element_type=F32)
        merged = term if merged is None else merged + term
    o_ref[...] = merged.astype(o_ref.dtype)


def _merge(hb, branches, w_in, w_branch, layer):
    n, d = hb.shape
    tm = min(MERGE_TM, n)
    tn = MERGE_TN
    nc = d // tn
    gate_off = COL_GATE // tn
    br_spec = pl.BlockSpec((tm, BR_WIDTH), lambda i, c: (i, 0))

    def gate_spec(b):
        return pl.BlockSpec((None, d, tn), lambda i, c: (layer, 0, gate_off + b * nc + c))

    return pl.pallas_call(
        _merge_kernel,
        grid=(n // tm, nc),
        in_specs=[
            pl.BlockSpec((tm, d), lambda i, c: (i, 0)),
            br_spec, br_spec, br_spec, br_spec,
            gate_spec(0), gate_spec(1), gate_spec(2), gate_spec(3),
            pl.BlockSpec((None, N_BRANCH, BR_WIDTH, tn), lambda i, c: (layer, 0, 0, c)),
        ],
        out_specs=pl.BlockSpec((tm, tn), lambda i, c: (i, c)),
        out_shape=jax.ShapeDtypeStruct((n, d), BF16),
        compiler_params=_params("parallel", "arbitrary"),
        name="merge",
    )(hb, *branches, w_in, w_in, w_in, w_in, w_branch)


def _out_ln_kernel(m_ref, w_ref, x_ref, g_ref, b_ref, o_ref):
    y = ALPHA * x_ref[...] + jnp.dot(m_ref[...], w_ref[...], preferred_element_type=F32)
    o_ref[...] = _layer_norm_rows(y, g_ref[...], b_ref[...])


def _out_ln(merged, w_out, layer, x, ln_g, ln_b):
    n, d = x.shape
    tm = min(OUT_TM, n)
    return pl.pallas_call(
        _out_ln_kernel,
        grid=(n // tm,),
        in_specs=[
            pl.BlockSpec((tm, d), lambda i: (i, 0)),
            pl.BlockSpec((None, d, d), lambda i: (layer, 0, 0)),
            pl.BlockSpec((tm, d), lambda i: (i, 0)),
            pl.BlockSpec((1, d), lambda i: (0, 0)),
            pl.BlockSpec((1, d), lambda i: (0, 0)),
        ],
        out_specs=pl.BlockSpec((tm, d), lambda i: (i, 0)),
        out_shape=jax.ShapeDtypeStruct((n, d), F32),
        compiler_params=_params("parallel"),
        name="out_proj_ln",
    )(merged, w_out, x, ln_g.reshape(1, d), ln_b.reshape(1, d))


def kernel(x, positions, ffn1_w_in, ffn1_w_out, ln1_g, ln1_b, w_in, gmlp_ln_g, gmlp_ln_b, gmlp_ws, gmlp_bs,
           diff_lq1, diff_lk1, diff_lq2, diff_lk2, diff_norm_g, conv_w, conv_b, conv_ln_g, conv_ln_b,
           w_branch, w_out, ln2_g, ln2_b, ffn2_w_in, ffn2_w_out, ln3_g, ln3_b):
    batch, seq, d = x.shape
    n = batch * seq
    depth = w_in.shape[0]
    ffn1_w_in, ffn1_w_out, ffn2_w_in, ffn2_w_out, w_in, w_branch, w_out = (
        w.astype(BF16) for w in (ffn1_w_in, ffn1_w_out, ffn2_w_in, ffn2_w_out, w_in, w_branch, w_out))

    rope = _rope_tables(positions)
    xf = x.reshape(n, d)
    for i in range(depth):
        lam_init = 0.8 - 0.6 * math.exp(-0.3 * i)
        xf, xb = _ffn(xf, ffn1_w_in, ffn1_w_out, i, ln1_g[i], ln1_b[i])

        proj = _proj(xb, w_in, i, rope)

        o_a = _gmlp(proj, gmlp_ln_g[i], gmlp_ln_b[i], gmlp_ws[i], gmlp_bs[i])
        o_b = _diff_attn(proj, diff_lq1[i], diff_lk1[i], diff_lq2[i], diff_lk2[i], diff_norm_g[i],
                         batch, seq, lam_init)
        o_c = _conv(proj, conv_w[i], conv_b[i], conv_ln_g[i], conv_ln_b[i], seq)
        o_d = _sb_attn(proj, batch, seq)

        merged = _merge(xb, (o_a, o_b, o_c, o_d), w_in, w_branch, i)
        xf = _out_ln(merged, w_out, i, xf, ln2_g[i], ln2_b[i])
        xf, _ = _ffn(xf, ffn2_w_in, ffn2_w_out, i, ln3_g[i], ln3_b[i])
    return xf.reshape(batch, seq, d)
```

```python
import functools
import math

import jax
import jax.numpy as jnp
from jax import lax
from jax.experimental import pallas as pl
from jax.experimental.pallas import tpu as pltpu

F32 = jnp.float32
BF16 = jnp.bfloat16

D_MODEL = 2048
FF_DIM = 5632
CHUNK = 64
N_BRANCH = 4
BR_WIDTH = 1024
GMLP_BLOCK = 128
GMLP_GROUPS = 4
GMLP_GC = BR_WIDTH // GMLP_GROUPS
DIFF_HEADS = 8
DIFF_D = 64
DIFF_VD = 2 * DIFF_D
SB_HEADS = 8
SB_D = 128
CONV_K = 31
ROT_DIM = DIFF_D // 4
ROPE_THETA = 500000.0
MODEL_DEPTH = 4
ALPHA = (2 * MODEL_DEPTH) ** 0.25
LN_EPS = 1e-5
COL_U = 0
COL_V = BR_WIDTH
COL_DQ = 2 * BR_WIDTH
COL_DK = 3 * BR_WIDTH
COL_DV = 4 * BR_WIDTH
COL_CA = 5 * BR_WIDTH
COL_CG = 6 * BR_WIDTH
COL_SQ = 7 * BR_WIDTH
COL_SK = 8 * BR_WIDTH
COL_SV = 9 * BR_WIDTH
COL_GATE = 10 * BR_WIDTH

LANES = 128
F32_SUBLANES = 8
BF16_SUBLANES = 16
VMEM_LIMIT_BYTES = 56 * 1024 * 1024

FFN_TM = 512
FFN_TF = 512
PROJ_TM = 1024
PROJ_TN = 1024
MERGE_TM = 512
MERGE_TN = 512
OUT_TM = 512
GMLP_TM = 512
CONV_TT = 256
CONV_HALO = 32
ATT_TQ = 512
SB_TK = 256
ROPE_TM = 1024


def _params(*sem):
    return pltpu.CompilerParams(dimension_semantics=sem, vmem_limit_bytes=VMEM_LIMIT_BYTES)


def _sigmoid(x):
    return 1.0 / (1.0 + jnp.exp(-x))


def _chunk_of(size, axis):
    return lax.broadcasted_iota(jnp.int32, (size, size), axis) >> int(math.log2(CHUNK))


def _layer_norm_rows(y, g, b):
    mu = jnp.mean(y, axis=-1, keepdims=True)
    yc = y - mu
    var = jnp.mean(yc * yc, axis=-1, keepdims=True)
    return yc * lax.rsqrt(var + LN_EPS) * g + b


def _ffn_kernel(x_ref, wa_ref, wg_ref, wo_ref, g_ref, b_ref, of_ref, ob_ref, xb_ref, acc_ref,
                act0_ref, act1_ref, *, nf):
    j = pl.program_id(1)
    acts = (act0_ref, act1_ref)

    def activate(dst_ref):
        xb = xb_ref[...]
        a = jnp.dot(xb, wa_ref[...], preferred_element_type=F32)
        gt = jnp.dot(xb, wg_ref[...], preferred_element_type=F32)
        dst_ref[...] = (a * _sigmoid(a) * gt).astype(BF16)

    def project_down(src_ref):
        acc_ref[...] += jnp.dot(src_ref[...], wo_ref[...], preferred_element_type=F32)

    @pl.when(j == 0)
    def _():
        xb_ref[...] = x_ref[...].astype(BF16)
        acc_ref[...] = jnp.zeros(acc_ref.shape, F32)
        activate(acts[0])

    for slot in range(2):
        @pl.when((j > 0) & (j < nf) & ((j & 1) == slot))
        def _(slot=slot):
            project_down(acts[1 - slot])
            activate(acts[slot])

    @pl.when(j == nf)
    def _():
        project_down(acts[(nf - 1) % 2])
        y = ALPHA * x_ref[...] + 0.5 * acc_ref[...]
        out = _layer_norm_rows(y, g_ref[...], b_ref[...])
        of_ref[...] = out
        ob_ref[...] = out.astype(BF16)


def _ffn(x, w_in, w_out, layer, ln_g, ln_b):
    n, d = x.shape
    f = w_out.shape[1]
    tm = min(FFN_TM, n)
    tf = FFN_TF
    nf = f // tf
    last = nf - 1
    return pl.pallas_call(
        functools.partial(_ffn_kernel, nf=nf),
        grid=(n // tm, nf + 1),
        in_specs=[
            pl.BlockSpec((tm, d), lambda i, j: (i, 0)),
            pl.BlockSpec((None, d, tf), lambda i, j: (layer, 0, jnp.minimum(j, last))),
            pl.BlockSpec((None, d, tf), lambda i, j: (layer, 0, jnp.minimum(j, last) + nf)),
            pl.BlockSpec((None, tf, d), lambda i, j: (layer, jnp.maximum(j - 1, 0), 0)),
            pl.BlockSpec((1, d), lambda i, j: (0, 0)),
            pl.BlockSpec((1, d), lambda i, j: (0, 0)),
        ],
        out_specs=[
            pl.BlockSpec((tm, d), lambda i, j: (i, 0)),
            pl.BlockSpec((tm, d), lambda i, j: (i, 0)),
        ],
        out_shape=[jax.ShapeDtypeStruct((n, d), F32), jax.ShapeDtypeStruct((n, d), BF16)],
        scratch_shapes=[pltpu.VMEM((tm, d), BF16), pltpu.VMEM((tm, d), F32),
                        pltpu.VMEM((tm, tf), BF16), pltpu.VMEM((tm, tf), BF16)],
        compiler_params=_params("parallel", "arbitrary"),
        name="ffn",
    )(x, w_in, w_in, w_out, ln_g.reshape(1, d), ln_b.reshape(1, d))


def _rope_table_kernel(pos_ref, invf_ref, c_ref, s1_ref, s2_ref):
    ang = pos_ref[...] * invf_ref[...]
    cos = jnp.cos(ang)
    sin = jnp.sin(ang)
    lane = lax.broadcasted_iota(jnp.int32, ang.shape, 1) & (DIFF_D - 1)
    half = ROT_DIM // 2
    c_ref[...] = jnp.where(lane < ROT_DIM, cos, 1.0)
    s1_ref[...] = jnp.where(lane < half, -sin, 0.0)
    s2_ref[...] = jnp.where((lane >= half) & (lane < ROT_DIM), sin, 0.0)


def _rope_tables(positions):
    n = positions.size
    tm = min(ROPE_TM, n)
    half = ROT_DIM // 2
    inv_freq = ROPE_THETA ** (-jnp.arange(0, ROT_DIM, 2, dtype=F32) / ROT_DIM)
    per_head = jnp.concatenate([inv_freq, inv_freq, jnp.zeros((DIFF_D - ROT_DIM,), F32)])
    invf = jnp.tile(per_head, LANES // DIFF_D).reshape(1, LANES)
    pos = positions.reshape(n, 1).astype(F32)
    tab = jax.ShapeDtypeStruct((n, LANES), F32)
    return pl.pallas_call(
        _rope_table_kernel,
        grid=(n // tm,),
        in_specs=[pl.BlockSpec((tm, 1), lambda i: (i, 0)), pl.BlockSpec((1, LANES), lambda i: (0, 0))],
        out_specs=[pl.BlockSpec((tm, LANES), lambda i: (i, 0))] * 3,
        out_shape=[tab, tab, tab],
        compiler_params=_params("parallel"),
        name="rope_tables",
    )(pos, invf)


def _proj_kernel(x_ref, w_ref, c_ref, s1_ref, s2_ref, o_ref, *, q_tile, k_tile, q_scale):
    j = pl.program_id(1)
    acc = jnp.dot(x_ref[...], w_ref[...], preferred_element_type=F32)
    rotary = (j == q_tile) | (j == k_tile)

    @pl.when(jnp.logical_not(rotary))
    def _():
        o_ref[...] = acc.astype(o_ref.dtype)

    @pl.when(rotary)
    def _():
        tn = acc.shape[1]
        reps = tn // LANES
        c = jnp.concatenate([c_ref[...]] * reps, axis=1)
        s1 = jnp.concatenate([s1_ref[...]] * reps, axis=1)
        s2 = jnp.concatenate([s2_ref[...]] * reps, axis=1)
        half = ROT_DIM // 2
        nxt = pltpu.roll(acc, tn - half, axis=1)
        prv = pltpu.roll(acc, half, axis=1)
        out = acc * c + nxt * s1 + prv * s2
        scale = jnp.where(j == q_tile, q_scale, 1.0).astype(F32)
        o_ref[...] = (out * scale).astype(o_ref.dtype)


def _proj(xb, w, layer, rope):
    n, d = xb.shape
    tm = min(PROJ_TM, n)
    tn = PROJ_TN
    t_spec = pl.BlockSpec((tm, LANES), lambda i, j: (i, 0))
    kern = functools.partial(_proj_kernel, q_tile=COL_DQ // tn, k_tile=COL_DK // tn, q_scale=DIFF_D ** -0.5)
    return pl.pallas_call(
        kern,
        grid=(n // tm, COL_GATE // tn),
        in_specs=[pl.BlockSpec((tm, d), lambda i, j: (i, 0)),
                  pl.BlockSpec((None, d, tn), lambda i, j: (layer, 0, j)),
                  t_spec, t_spec, t_spec],
        out_specs=pl.BlockSpec((tm, tn), lambda i, j: (i, j)),
        out_shape=jax.ShapeDtypeStruct((n, COL_GATE), BF16),
        compiler_params=_params("parallel", "arbitrary"),
        name="proj",
    )(xb, w, *rope)


def _gmlp_kernel(u_ref, v_ref, g_ref, b_ref, ws_ref, bs_ref, o_ref):
    tm = u_ref.shape[0]
    visible = _chunk_of(GMLP_BLOCK, 1) <= _chunk_of(GMLP_BLOCK, 0)
    g = g_ref[...]
    b = b_ref[...]
    for blk in range(tm // GMLP_BLOCK):
        rows = pl.ds(blk * GMLP_BLOCK, GMLP_BLOCK)
        vn = _layer_norm_rows(v_ref[rows, :].astype(F32), g, b).astype(BF16)
        for grp in range(GMLP_GROUPS):
            cols = pl.ds(grp * GMLP_GC, GMLP_GC)
            w = jnp.where(visible, ws_ref[grp], 0.0).astype(BF16)
            mixed = jnp.dot(w, vn[:, grp * GMLP_GC:(grp + 1) * GMLP_GC], preferred_element_type=F32)
            mixed = mixed + bs_ref[:, grp:grp + 1]
            o_ref[rows, cols] = (u_ref[rows, cols].astype(F32) * mixed).astype(o_ref.dtype)


def _gmlp(proj, ln_g, ln_b, ws, bs):
    n = proj.shape[0]
    tm = min(GMLP_TM, n)
    return pl.pallas_call(
        _gmlp_kernel,
        grid=(n // tm,),
        in_specs=[
            pl.BlockSpec((tm, BR_WIDTH), lambda i: (i, COL_U // BR_WIDTH)),
            pl.BlockSpec((tm, BR_WIDTH), lambda i: (i, COL_V // BR_WIDTH)),
            pl.BlockSpec((1, BR_WIDTH), lambda i: (0, 0)),
            pl.BlockSpec((1, BR_WIDTH), lambda i: (0, 0)),
            pl.BlockSpec((GMLP_GROUPS, GMLP_BLOCK, GMLP_BLOCK), lambda i: (0, 0, 0)),
            pl.BlockSpec((GMLP_BLOCK, GMLP_GROUPS), lambda i: (0, 0)),
        ],
        out_specs=pl.BlockSpec((tm, BR_WIDTH), lambda i: (i, 0)),
        out_shape=jax.ShapeDtypeStruct((n, BR_WIDTH), BF16),
        compiler_params=_params("parallel"),
        name="gmlp",
    )(proj, proj, ln_g.reshape(1, -1), ln_b.reshape(1, -1), ws, bs.T)


def _store_transposed_blocks(v_ref, vt_ref):
    nblk, rows, tk = vt_ref.shape
    d = v_ref.shape[1]
    for c in range(nblk):
        blk = v_ref[pl.ds(c * tk, tk), :].astype(F32)
        vt_ref[c, pl.ds(0, d), :] = blk.T.astype(vt_ref.dtype)
        if rows > d:
            first = lax.broadcasted_iota(jnp.int32, (rows - d, tk), 0) == 0
            vt_ref[c, pl.ds(d, rows - d), :] = jnp.where(first, 1.0, 0.0).astype(vt_ref.dtype)


def _diff_attn_kernel(q_ref, k_ref, v_ref, lq1_ref, lk1_ref, lq2_ref, lk2_ref, ng_ref, o_ref,
                      vt_ref, s0_ref, s1_ref, bmax0_ref, bmax1_ref, m_ref, acc_ref, *, lam_init):
    qi = pl.program_id(2)
    tq = q_ref.shape[0]
    q = q_ref[...]
    lane = lax.broadcasted_iota(jnp.int32, q.shape, 1)
    q_both = jnp.concatenate([jnp.where(lane < DIFF_D, q, 0), jnp.where(lane >= DIFF_D, q, 0)], axis=0)

    @pl.when(qi == 0)
    def _():
        _store_transposed_blocks(v_ref, vt_ref)

    m_ref[...] = jnp.full(m_ref.shape, -1e30, F32)
    acc_ref[...] = jnp.zeros(acc_ref.shape, F32)

    s_slots = ((s0_ref, bmax0_ref), (s1_ref, bmax1_ref))

    def scores(j, slot):
        s_ref, bmax_ref = s_slots[slot]
        kb = k_ref[pl.ds(pl.multiple_of(j * tq, tq), tq), :]
        s = lax.dot_general(kb, q_both, (((1,), (1,)), ((), ())), preferred_element_type=F32)
        s_ref[...] = s
        bmax_ref[...] = jnp.max(s, axis=0, keepdims=True)

    def softmax_pv(j, slot, visible):
        s_ref, bmax_ref = s_slots[slot]
        s = s_ref[...]
        if visible is None:
            block_max = bmax_ref[...]
        else:
            s = jnp.where(visible, s, -jnp.inf)
            block_max = jnp.max(s, axis=0, keepdims=True)
        m_old = m_ref[...]
        m_new = jnp.maximum(m_old, block_max)
        p = jnp.exp(s - m_new)
        corr = jnp.exp(m_old - m_new)
        acc_ref[...] = corr * acc_ref[...] + jnp.dot(vt_ref[j], p.astype(BF16), preferred_element_type=F32)
        m_ref[...] = m_new

    scores(0, 0)

    def body(j, carry):
        for slot in range(2):
            @pl.when((j & 1) == slot)
            def _(slot=slot):
                scores(j + 1, 1 - slot)
                softmax_pv(j, slot, None)
        return carry

    lax.fori_loop(0, qi, body, 0)
    visible = _chunk_of(tq, 0) <= _chunk_of(tq, 1)
    visible = jnp.concatenate([visible, visible], axis=1)
    for slot in range(2):
        @pl.when((qi & 1) == slot)
        def _(slot=slot):
            softmax_pv(qi, slot, visible)

    lam = (jnp.exp(jnp.sum(lq1_ref[...] * lk1_ref[...], axis=-1, keepdims=True))
           - jnp.exp(jnp.sum(lq2_ref[...] * lk2_ref[...], axis=-1, keepdims=True)) + lam_init)
    denom = acc_ref[pl.ds(DIFF_VD, 1), :]
    o_both = acc_ref[pl.ds(0, DIFF_VD), :] * (1.0 / denom)
    ot = o_both[:, :tq] - lam * o_both[:, tq:]
    ot = ot * lax.rsqrt(jnp.mean(ot * ot, axis=0, keepdims=True) + LN_EPS)
    o_ref[...] = (ot.T * ng_ref[...] * (1.0 - lam_init)).astype(o_ref.dtype)


def _diff_attn(proj, lq1, lk1, lq2, lk2, norm_g, batch, seq, lam_init):
    n = proj.shape[0]
    tq = min(ATT_TQ, seq)
    nq = seq // tq
    vec = pl.BlockSpec((1, DIFF_D), lambda b, h, i: (0, 0))
    q0, k0, v0 = COL_DQ // DIFF_VD, COL_DK // DIFF_VD, COL_DV // DIFF_VD
    return pl.pallas_call(
        functools.partial(_diff_attn_kernel, lam_init=lam_init),
        grid=(batch, DIFF_HEADS, nq),
        in_specs=[
            pl.BlockSpec((tq, DIFF_VD), lambda b, h, i: (b * nq + i, q0 + h)),
            pl.BlockSpec((seq, DIFF_VD), lambda b, h, i: (b, k0 + h)),
            pl.BlockSpec((seq, DIFF_VD), lambda b, h, i: (b, v0 + h)),
            vec, vec, vec, vec,
            pl.BlockSpec((1, DIFF_VD), lambda b, h, i: (0, h)),
        ],
        out_specs=pl.BlockSpec((tq, DIFF_VD), lambda b, h, i: (b * nq + i, h)),
        out_shape=jax.ShapeDtypeStruct((n, DIFF_HEADS * DIFF_VD), BF16),
        scratch_shapes=[pltpu.VMEM((nq, DIFF_VD + BF16_SUBLANES, tq), BF16),
                        pltpu.VMEM((tq, 2 * tq), F32), pltpu.VMEM((tq, 2 * tq), F32),
                        pltpu.VMEM((1, 2 * tq), F32), pltpu.VMEM((1, 2 * tq), F32),
                        pltpu.VMEM((1, 2 * tq), F32),
                        pltpu.VMEM((DIFF_VD + BF16_SUBLANES, 2 * tq), F32)],
        compiler_params=_params("parallel", "parallel", "arbitrary"),
        name="diff_attn",
    )(proj, proj, proj, lq1.reshape(1, -1), lk1.reshape(1, -1), lq2.reshape(1, -1), lk2.reshape(1, -1),
      norm_g.reshape(1, -1))


def _conv_kernel(a_ref, g_ref, ah_ref, gh_ref, w_ref, cb_ref, lg_ref, lb_ref, o_ref, h_ref, y_ref,
                 *, tiles_per_seq):
    tt = a_ref.shape[0]
    first = (pl.program_id(0) % tiles_per_seq) == 0
    halo = ah_ref[...].astype(F32) * _sigmoid(gh_ref[...].astype(F32))
    h_ref[pl.ds(0, CONV_HALO), :] = jnp.where(first, 0.0, halo)
    h_ref[pl.ds(CONV_HALO, tt), :] = a_ref[...].astype(F32) * _sigmoid(g_ref[...].astype(F32))
    base = CONV_HALO - (CONV_K - 1)
    rows = tt + CONV_HALO
    for cb in range(a_ref.shape[1] // LANES):
        cols = pl.ds(cb * LANES, LANES)
        hc = h_ref[:, cols]
        acc = jnp.zeros((tt, LANES), F32)
        for r in range(F32_SUBLANES):
            shifted = pltpu.roll(hc, rows - (base + r), axis=0)
            for k in range(r, CONV_K, F32_SUBLANES):
                acc = acc + w_ref[pl.ds(k, 1), cols] * shifted[k - r:k - r + tt, :]
        y_ref[:, cols] = acc + cb_ref[:, cols]
    y = _layer_norm_rows(y_ref[...], lg_ref[...], lb_ref[...])
    o_ref[...] = (y * _sigmoid(y)).astype(o_ref.dtype)


def _conv(proj, w_dw, b_dw, ln_g, ln_b, seq):
    n = proj.shape[0]
    c = BR_WIDTH
    tt = min(CONV_TT, seq)
    ratio = tt // CONV_HALO
    row = lambda v: v.reshape(1, c)
    ca, cg = COL_CA // c, COL_CG // c
    return pl.pallas_call(
        functools.partial(_conv_kernel, tiles_per_seq=seq // tt),
        grid=(n // tt,),
        in_specs=[
            pl.BlockSpec((tt, c), lambda i: (i, ca)),
            pl.BlockSpec((tt, c), lambda i: (i, cg)),
            pl.BlockSpec((CONV_HALO, c), lambda i: (jnp.maximum(i * ratio - 1, 0), ca)),
            pl.BlockSpec((CONV_HALO, c), lambda i: (jnp.maximum(i * ratio - 1, 0), cg)),
            pl.BlockSpec((CONV_K, c), lambda i: (0, 0)),
            pl.BlockSpec((1, c), lambda i: (0, 0)),
            pl.BlockSpec((1, c), lambda i: (0, 0)),
            pl.BlockSpec((1, c), lambda i: (0, 0)),
        ],
        out_specs=pl.BlockSpec((tt, c), lambda i: (i, 0)),
        out_shape=jax.ShapeDtypeStruct((n, c), BF16),
        scratch_shapes=[pltpu.VMEM((tt + CONV_HALO, c), F32), pltpu.VMEM((tt, c), F32)],
        compiler_params=_params("parallel"),
        name="conformer_conv",
    )(proj, proj, proj, proj, w_dw, row(b_dw), row(ln_g), row(ln_b))


def _sb_attn_kernel(q_ref, k_ref, v_ref, o_ref, vt_ref, z0_ref, z1_ref, logw0_ref, logw1_ref,
                    tot0_ref, tot1_ref, rest_ref, acc_ref, *, tk):
    qi = pl.program_id(2)
    tq = q_ref.shape[0]
    q = q_ref[...]

    @pl.when(qi == 0)
    def _():
        _store_transposed_blocks(v_ref, vt_ref)

    to_log2 = SB_D ** -0.5 * math.log2(math.e)
    key_a = lax.broadcasted_iota(jnp.int32, (tk, tk), 0)
    key_b = lax.broadcasted_iota(jnp.int32, (tk, tk), 1)
    later = jnp.where(key_b > key_a, 1.0, 0.0).astype(BF16)

    rest_ref[...] = jnp.zeros(rest_ref.shape, F32)
    acc_ref[...] = jnp.zeros(acc_ref.shape, F32)
    z_slots = (z0_ref, z1_ref)
    slots = ((logw0_ref, tot0_ref), (logw1_ref, tot1_ref))

    def scores(p, slot):
        j = qi - p
        kb = k_ref[pl.ds(pl.multiple_of(j * tq, tq), tq), :]
        z_slots[slot][...] = lax.dot_general(kb, q, (((1,), (1,)), ((), ())),
                                             preferred_element_type=F32) * to_log2

    def prepare(slot, diagonal):
        logw_ref, tot_ref = slots[slot]
        z = z_slots[slot][...]
        log_rest = -(jnp.maximum(z, 0.0) + jnp.log2(1.0 + jnp.exp2(-jnp.abs(z))))
        if diagonal:
            mask = (lax.broadcasted_iota(jnp.int32, (tq, tq), 0)
                    < lax.broadcasted_iota(jnp.int32, (tq, tq), 1))
            log_rest = jnp.where(mask, log_rest, 0.0)
        terms = log_rest.astype(BF16)
        running = jnp.zeros((1, tq), F32)
        for sub in reversed(range(tq // tk)):
            rows = slice(sub * tk, (sub + 1) * tk)
            suffix = jnp.dot(later, terms[rows], preferred_element_type=F32)
            logw = (z[rows] + log_rest[rows]) + suffix + running
            if diagonal:
                logw = jnp.where(mask[rows], logw, -1e30)
            logw_ref[rows, :] = logw
            running = running + jnp.sum(log_rest[rows], axis=0, keepdims=True)
        tot_ref[...] = running

    def accumulate(p, slot):
        logw_ref, tot_ref = slots[slot]
        rest = rest_ref[...]
        w = jnp.exp2(logw_ref[...] + rest)
        acc_ref[...] += jnp.dot(vt_ref[qi - p], w.astype(BF16), preferred_element_type=F32)
        rest_ref[...] = rest + tot_ref[...]

    nblk = qi + 1
    scores(0, 0)

    @pl.when(nblk >= 2)
    def _():
        scores(1, 1)

    prepare(0, True)

    def body(t, carry):
        for slot in range(2):
            @pl.when((t & 1) == slot)
            def _(slot=slot):
                scores(t, slot)
                prepare(1 - slot, False)
                accumulate(t - 2, slot)
        return carry

    lax.fori_loop(2, nblk, body, 0)
    for slot in range(2):
        @pl.when((nblk >= 2) & (((nblk - 1) & 1) == slot))
        def _(slot=slot):
            prepare(slot, False)
            accumulate(nblk - 2, 1 - slot)
    for slot in range(2):
        @pl.when(((nblk - 1) & 1) == slot)
        def _(slot=slot):
            accumulate(nblk - 1, slot)
    o_ref[...] = acc_ref[...].T.astype(o_ref.dtype)


def _sb_attn(proj, batch, seq):
    n = proj.shape[0]
    tq = min(ATT_TQ, seq)
    nq = seq // tq
    tk = min(SB_TK, tq)
    q0, k0, v0 = COL_SQ // SB_D, COL_SK // SB_D, COL_SV // SB_D
    return pl.pallas_call(
        functools.partial(_sb_attn_kernel, tk=tk),
        grid=(batch, SB_HEADS, nq),
        in_specs=[
            pl.BlockSpec((tq, SB_D), lambda b, h, i: (b * nq + i, q0 + h)),
            pl.BlockSpec((seq, SB_D), lambda b, h, i: (b, k0 + h)),
            pl.BlockSpec((seq, SB_D), lambda b, h, i: (b, v0 + h)),
        ],
        out_specs=pl.BlockSpec((tq, SB_D), lambda b, h, i: (b * nq + i, h)),
        out_shape=jax.ShapeDtypeStruct((n, SB_HEADS * SB_D), BF16),
        scratch_shapes=[pltpu.VMEM((nq, SB_D, tq), BF16),
                        pltpu.VMEM((tq, tq), F32), pltpu.VMEM((tq, tq), F32),
                        pltpu.VMEM((tq, tq), F32), pltpu.VMEM((tq, tq), F32),
                        pltpu.VMEM((1, tq), F32), pltpu.VMEM((1, tq), F32),
                        pltpu.VMEM((1, tq), F32), pltpu.VMEM((SB_D, tq), F32)],
        compiler_params=_params("parallel", "parallel", "arbitrary"),
        name="sb_attn",
    )(proj, proj, proj)


def _merge_kernel(h_ref, oa_ref, ob_ref, oc_ref, od_ref, wga_ref, wgb_ref, wgc_ref, wgd_ref, wb_ref, o_ref):
    h = h_ref[...]
    merged = None
    for idx, (br_ref, wg_ref) in enumerate(((oa_ref, wga_ref), (ob_ref, wgb_ref),
                                            (oc_ref, wgc_ref), (od_ref, wgd_ref))):
        gate = _sigmoid(jnp.dot(h, wg_ref[...], preferred_element_type=F32))
        term = gate * jnp.dot(br_ref[...], wb_ref[idx], preferred_element_type=F32)
        merged = term if merged is None else merged + term
    o_ref[...] = merged.astype(o_ref.dtype)


def _merge(hb, branches, w_in, w_branch, layer):
    n, d = hb.shape
    tm = min(MERGE_TM, n)
    tn = MERGE_TN
    nc = d // tn
    gate_off = COL_GATE // tn
    br_spec = pl.BlockSpec((tm, BR_WIDTH), lambda i, c: (i, 0))

    def gate_spec(b):
        return pl.BlockSpec((None, d, tn), lambda i, c: (layer, 0, gate_off + b * nc + c))

    return pl.pallas_call(
        _merge_kernel,
        grid=(n // tm, nc),
        in_specs=[
            pl.BlockSpec((tm, d), lambda i, c: (i, 0)),
            br_spec, br_spec, br_spec, br_spec,
            gate_spec(0), gate_spec(1), gate_spec(2), gate_spec(3),
            pl.BlockSpec((None, N_BRANCH, BR_WIDTH, tn), lambda i, c: (layer, 0, 0, c)),
        ],
        out_specs=pl.BlockSpec((tm, tn), lambda i, c: (i, c)),
        out_shape=jax.ShapeDtypeStruct((n, d), BF16),
        compiler_params=_params("parallel", "arbitrary"),
        name="merge",
    )(hb, *branches, w_in, w_in, w_in, w_in, w_branch)


def _out_ln_kernel(m_ref, w_ref, x_ref, g_ref, b_ref, o_ref):
    y = ALPHA * x_ref[...] + jnp.dot(m_ref[...], w_ref[...], preferred_element_type=F32)
    o_ref[...] = _layer_norm_rows(y, g_ref[...], b_ref[...])


def _out_ln(merged, w_out, layer, x, ln_g, ln_b):
    n, d = x.shape
    tm = min(OUT_TM, n)
    return pl.pallas_call(
        _out_ln_kernel,
        grid=(n // tm,),
        in_specs=[
            pl.BlockSpec((tm, d), lambda i: (i, 0)),
            pl.BlockSpec((None, d, d), lambda i: (layer, 0, 0)),
            pl.BlockSpec((tm, d), lambda i: (i, 0)),
            pl.BlockSpec((1, d), lambda i: (0, 0)),
            pl.BlockSpec((1, d), lambda i: (0, 0)),
        ],
        out_specs=pl.BlockSpec((tm, d), lambda i: (i, 0)),
        out_shape=jax.ShapeDtypeStruct((n, d), F32),
        compiler_params=_params("parallel"),
        name="out_proj_ln",
    )(merged, w_out, x, ln_g.reshape(1, d), ln_b.reshape(1, d))


def kernel(x, positions, ffn1_w_in, ffn1_w_out, ln1_g, ln1_b, w_in, gmlp_ln_g, gmlp_ln_b, gmlp_ws, gmlp_bs,
           diff_lq1, diff_lk1, diff_lq2, diff_lk2, diff_norm_g, conv_w, conv_b, conv_ln_g, conv_ln_b,
           w_branch, w_out, ln2_g, ln2_b, ffn2_w_in, ffn2_w_out, ln3_g, ln3_b):
    batch, seq, d = x.shape
    n = batch * seq
    depth = w_in.shape[0]
    ffn1_w_in, ffn1_w_out, ffn2_w_in, ffn2_w_out, w_in, w_branch, w_out = (
        w.astype(BF16) for w in (ffn1_w_in, ffn1_w_out, ffn2_w_in, ffn2_w_out, w_in, w_branch, w_out))

    rope = _rope_tables(positions)
    xf = x.reshape(n, d)
    for i in range(depth):
        lam_init = 0.8 - 0.6 * math.exp(-0.3 * i)
        xf, xb = _ffn(xf, ffn1_w_in, ffn1_w_out, i, ln1_g[i], ln1_b[i])

        proj = _proj(xb, w_in, i, rope)

        o_a = _gmlp(proj, gmlp_ln_g[i], gmlp_ln_b[i], gmlp_ws[i], gmlp_bs[i])
        o_b = _diff_attn(proj, diff_lq1[i], diff_lk1[i], diff_lq2[i], diff_lk2[i], diff_norm_g[i],
                         batch, seq, lam_init)
        o_c = _conv(proj, conv_w[i], conv_b[i], conv_ln_g[i], conv_ln_b[i], seq)
        o_d = _sb_attn(proj, batch, seq)

        merged = _merge(xb, (o_a, o_b, o_c, o_d), w_in, w_branch, i)
        xf = _out_ln(merged, w_out, i, xf, ln2_g[i], ln2_b[i])
        xf, _ = _ffn(xf, ffn2_w_in, ffn2_w_out, i, ln3_g[i], ln3_b[i])
    return xf.reshape(batch, seq, d)
```
